```python
import math
import jax, jax.numpy as jnp
from jax import lax
import numpy as np

D_MODEL = 2048
BATCH = 1
SEQ = 8192
DEPTH = 1
DEC_BATCH = 16
DEC_SEQ = 2048
PAST_LEN = 128

MIX_WIDTH = D_MODEL
MLA_HEADS = 8
MLA_V_DIM = 128
MLA_NOPE_DIM = 128
MLA_ROPE_DIM = 64
Q_LORA = 512
KV_LORA = 512
MLA_WIDTH = MLA_HEADS * MLA_V_DIM
GMLP_GROUPS = 8
GMLP_GROUP_DIM = 128
GMLP_WIDTH = GMLP_GROUPS * GMLP_GROUP_DIM
CHUNK = 128
IN_WIDTH = Q_LORA + KV_LORA + MLA_ROPE_DIM + 2 * GMLP_WIDTH
D_FF = 5632
Q_BLOCK = 128
ROPE_THETA = 10000.0
EPS = 1e-6

kernel_name = "hybrid_mla_gmlp_macaron_encoder"


def rms_norm(x, g):
    xf = x.astype(jnp.float32)
    y = xf * lax.rsqrt(jnp.mean(xf * xf, axis=-1, keepdims=True) + EPS)
    return (y * g.astype(jnp.float32)).astype(x.dtype)


def swiglu(h, w_gate, w_up, w_down):
    return (jax.nn.silu(h @ w_gate) * (h @ w_up)) @ w_down


def rope_tables(seq):
    inv = 1.0 / (ROPE_THETA ** (jnp.arange(0, MLA_ROPE_DIM, 2, dtype=jnp.float32) / MLA_ROPE_DIM))
    ang = jnp.arange(seq, dtype=jnp.float32)[:, None] * inv[None, :]
    return jnp.cos(ang), jnp.sin(ang)


def apply_rope(x, cos, sin):
    half = x.shape[-1] // 2
    x1 = x[..., :half].astype(jnp.float32)
    x2 = x[..., half:].astype(jnp.float32)
    out = jnp.concatenate([x1 * cos - x2 * sin, x1 * sin + x2 * cos], axis=-1)
    return out.astype(x.dtype)


def mla_attention(q_nope, q_rope, k_nope, k_rope, v):
    B, S, H, _ = q_nope.shape
    nblk = S // Q_BLOCK
    scale = 1.0 / math.sqrt(MLA_NOPE_DIM + MLA_ROPE_DIM)
    qn = jnp.moveaxis(q_nope.reshape(B, nblk, Q_BLOCK, H, MLA_NOPE_DIM), 1, 0)
    qr = jnp.moveaxis(q_rope.reshape(B, nblk, Q_BLOCK, H, MLA_ROPE_DIM), 1, 0)

    def block(args):
        qn_b, qr_b = args
        s = (jnp.einsum('bqhd,bkhd->bhqk', qn_b, k_nope, preferred_element_type=jnp.float32)
             + jnp.einsum('bqhd,bkd->bhqk', qr_b, k_rope, preferred_element_type=jnp.float32))
        p = jax.nn.softmax(s * scale, axis=-1).astype(v.dtype)
        return jnp.einsum('bhqk,bkhd->bqhd', p, v)

    o = lax.map(block, (qn, qr))
    return jnp.moveaxis(o, 0, 1).reshape(B, S, H * MLA_V_DIM)


def chunked_spatial_gating(u, v, g_v, w_s, b_s):
    B, S, _ = u.shape
    n = S // CHUNK
    vn = rms_norm(v.reshape(B, S, GMLP_GROUPS, GMLP_GROUP_DIM),
                  g_v.reshape(GMLP_GROUPS, GMLP_GROUP_DIM))
    vn = vn.reshape(B, n, CHUNK, GMLP_GROUPS, GMLP_GROUP_DIM)
    s = jnp.einsum('gij,bnjgc->bnigc', w_s, vn) + jnp.transpose(b_s)[None, None, :, :, None]
    return u * s.reshape(B, S, GMLP_WIDTH)


def encoder_layer(x, g_ffn1, w1_gate, w1_up, w1_down, g_mix, w_in, g_q, w_q_b, g_kv, w_kv_b,
                  g_v, w_s, b_s, g_out_attn, g_out_gmlp, w_out, g_ffn2, w2_gate, w2_up, w2_down):
    B, S, _ = x.shape
    x = x + 0.5 * swiglu(rms_norm(x, g_ffn1), w1_gate, w1_up, w1_down)

    z = rms_norm(x, g_mix) @ w_in
    o1 = Q_LORA
    o2 = o1 + KV_LORA
    o3 = o2 + MLA_ROPE_DIM
    o4 = o3 + GMLP_WIDTH
    c_q, c_kv, k_rope, u, v = z[..., :o1], z[..., o1:o2], z[..., o2:o3], z[..., o3:o4], z[..., o4:]

    cos, sin = rope_tables(S)
    q = (rms_norm(c_q, g_q) @ w_q_b).reshape(B, S, MLA_HEADS, MLA_NOPE_DIM + MLA_ROPE_DIM)
    q_nope = q[..., :MLA_NOPE_DIM]
    q_rope = apply_rope(q[..., MLA_NOPE_DIM:], cos[None, :, None, :], sin[None, :, None, :])
    kv = (rms_norm(c_kv, g_kv) @ w_kv_b).reshape(B, S, MLA_HEADS, MLA_NOPE_DIM + MLA_V_DIM)
    k_nope = kv[..., :MLA_NOPE_DIM]
    v_att = kv[..., MLA_NOPE_DIM:]
    k_rope = apply_rope(k_rope, cos[None], sin[None])
    o_attn = mla_attention(q_nope, q_rope, k_nope, k_rope, v_att)

    o_gmlp = chunked_spatial_gating(jax.nn.gelu(u), jax.nn.gelu(v), g_v, w_s, b_s)

    o = jnp.concatenate([rms_norm(o_attn, g_out_attn), rms_norm(o_gmlp, g_out_gmlp)], axis=-1)
    x = x + o @ w_out

    x = x + 0.5 * swiglu(rms_norm(x, g_ffn2), w2_gate, w2_up, w2_down)
    return x


def setup_inputs(seed: int = 0) -> dict:
    key = jax.random.key(seed)
    ks = iter(jax.random.split(key, 32))
    f32 = jnp.float32

    def nrm(shape, fan_in):
        return jax.random.normal(next(ks), shape, f32) * (fan_in ** -0.5)

    def gain(shape):
        return 1.0 + 0.01 * jax.random.normal(next(ks), shape, f32)

    L = DEPTH
    H = MLA_HEADS
    return {
        "x_prompt": jax.random.normal(next(ks), (BATCH, SEQ, D_MODEL), f32),
        "x_sample": jax.random.normal(next(ks), (DEC_BATCH, DEC_SEQ, D_MODEL), f32),
        "g_ffn1": gain((L, D_MODEL)),
        "w1_gate": nrm((L, D_MODEL, D_FF), D_MODEL),
        "w1_up": nrm((L, D_MODEL, D_FF), D_MODEL),
        "w1_down": nrm((L, D_FF, D_MODEL), D_FF),
        "g_mix": gain((L, D_MODEL)),
        "w_in": nrm((L, D_MODEL, IN_WIDTH), D_MODEL),
        "g_q": gain((L, Q_LORA)),
        "w_q_b": nrm((L, Q_LORA, H * (MLA_NOPE_DIM + MLA_ROPE_DIM)), Q_LORA),
        "g_kv": gain((L, KV_LORA)),
        "w_kv_b": nrm((L, KV_LORA, H * (MLA_NOPE_DIM + MLA_V_DIM)), KV_LORA),
        "g_v": gain((L, GMLP_WIDTH)),
        "w_s": nrm((L, GMLP_GROUPS, CHUNK, CHUNK), CHUNK),
        "b_s": 1.0 + 0.01 * jax.random.normal(next(ks), (L, GMLP_GROUPS, CHUNK), f32),
        "g_out_attn": gain((L, MLA_WIDTH)),
        "g_out_gmlp": gain((L, GMLP_WIDTH)),
        "w_out": nrm((L, MIX_WIDTH, D_MODEL), MIX_WIDTH),
        "g_ffn2": gain((L, D_MODEL)),
        "w2_gate": nrm((L, D_MODEL, D_FF), D_MODEL),
        "w2_up": nrm((L, D_MODEL, D_FF), D_MODEL),
        "w2_down": nrm((L, D_FF, D_MODEL), D_FF),
        "g_final": gain((D_MODEL,)),
    }


def _trunk(x, g_ffn1, w1_gate, w1_up, w1_down, g_mix, w_in, g_q, w_q_b, g_kv, w_kv_b,
           g_v, w_s, b_s, g_out_attn, g_out_gmlp, w_out, g_ffn2, w2_gate, w2_up, w2_down, g_final):
    for l in range(DEPTH):
        x = encoder_layer(x, g_ffn1[l], w1_gate[l], w1_up[l], w1_down[l], g_mix[l], w_in[l],
                          g_q[l], w_q_b[l], g_kv[l], w_kv_b[l], g_v[l], w_s[l], b_s[l],
                          g_out_attn[l], g_out_gmlp[l], w_out[l], g_ffn2[l],
                          w2_gate[l], w2_up[l], w2_down[l])
    return rms_norm(x, g_final)


def reference(x_prompt, x_sample, g_ffn1, w1_gate, w1_up, w1_down, g_mix, w_in, g_q, w_q_b,
              g_kv, w_kv_b, g_v, w_s, b_s, g_out_attn, g_out_gmlp, w_out, g_ffn2,
              w2_gate, w2_up, w2_down, g_final):
    y_prompt = _trunk(x_prompt, g_ffn1, w1_gate, w1_up, w1_down, g_mix, w_in, g_q, w_q_b,
                      g_kv, w_kv_b, g_v, w_s, b_s, g_out_attn, g_out_gmlp, w_out, g_ffn2,
                      w2_gate, w2_up, w2_down, g_final)
    y_sample = _trunk(x_sample, g_ffn1, w1_gate, w1_up, w1_down, g_mix, w_in, g_q, w_q_b,
                      g_kv, w_kv_b, g_v, w_s, b_s, g_out_attn, g_out_gmlp, w_out, g_ffn2,
                      w2_gate, w2_up, w2_down, g_final)
    return (y_prompt, y_sample)
```

```python
import functools
import math

import jax
import jax.numpy as jnp
from jax import lax
from jax.experimental import pallas as pl
from jax.experimental.pallas import tpu as pltpu

F32 = jnp.float32
BF16 = jnp.bfloat16

D_MODEL = 2048
HEADS = 8
NOPE = 128
ROPE = 64
HALF_ROPE = ROPE // 2
V_DIM = 128
Q_LORA = 512
KV_LORA = 512
GROUPS = 8
GROUP_DIM = 128
GMLP_WIDTH = GROUPS * GROUP_DIM
MLA_WIDTH = HEADS * V_DIM
CHUNK = 128
D_FF = 5632
ROPE_THETA = 10000.0
EPS = 1e-6

LANES = 128
QK_PAD = 2 * LANES
C_Q0, C_KV0 = 0, Q_LORA
C_KR0 = Q_LORA + KV_LORA
C_U0 = C_KR0 + 2 * LANES
C_V0 = C_U0 + GMLP_WIDTH
IN_COLS = C_V0 + GMLP_WIDTH
Q_COLS = HEADS * QK_PAD + HEADS * LANES

VMEM_LIMIT_BYTES = 60000 * 1024

FFN_TM, FFN_TF = 512, 512
MIX_TM = 256
MERGE_TM = 512


def _rms(x, g):
    return x * lax.rsqrt(jnp.mean(x * x, axis=-1, keepdims=True) + EPS) * g


def _dot(a, b):
    return jnp.dot(a, b, preferred_element_type=F32)


def _params(*sem):
    return pltpu.CompilerParams(dimension_semantics=sem, vmem_limit_bytes=VMEM_LIMIT_BYTES)


def _resident(shape):
    return pl.BlockSpec(shape, lambda *_: (0,) * len(shape), pipeline_mode=pl.Buffered(1))


def _ffn_body(final_norm, x_ref, g_ref, wg_ref, wu_ref, wd_ref, *rest):
    if final_norm:
        gf_ref, o_ref, h_ref = rest
    else:
        o_ref, h_ref = rest
    j = pl.program_id(1)

    @pl.when(j == 0)
    def _():
        x = x_ref[...]
        h_ref[...] = _rms(x, g_ref[...]).astype(BF16)
        o_ref[...] = x

    h = h_ref[...]
    gate = _dot(h, wg_ref[...])
    up = _dot(h, wu_ref[...])
    act = (0.5 * gate * jax.nn.sigmoid(gate) * up).astype(BF16)
    o_ref[...] += _dot(act, wd_ref[...])

    if final_norm:
        @pl.when(j == pl.num_programs(1) - 1)
        def _():
            o_ref[...] = _rms(o_ref[...], gf_ref[...])


def _ffn(x, g, wg, wu, wd, g_final=None):
    t = x.shape[0]
    final_norm = g_final is not None
    row = pl.BlockSpec((FFN_TM, D_MODEL), lambda i, j: (i, 0))
    gain = pl.BlockSpec((1, D_MODEL), lambda i, j: (0, 0))
    in_specs = [row, gain,
                pl.BlockSpec((D_MODEL, FFN_TF), lambda i, j: (0, j)),
                pl.BlockSpec((D_MODEL, FFN_TF), lambda i, j: (0, j)),
                pl.BlockSpec((FFN_TF, D_MODEL), lambda i, j: (j, 0))]
    args = [x, g, wg, wu, wd]
    if final_norm:
        in_specs.append(gain)
        args.append(g_final)
    return pl.pallas_call(
        functools.partial(_ffn_body, final_norm),
        grid=(t // FFN_TM, D_FF // FFN_TF),
        in_specs=in_specs,
        out_specs=row,
        out_shape=jax.ShapeDtypeStruct((t, D_MODEL), F32),
        scratch_shapes=[pltpu.VMEM((FFN_TM, D_MODEL), BF16)],
        compiler_params=_params("parallel", "arbitrary"),
        name="ffn_final" if final_norm else "ffn",
    )(*args)


def _mix_body(x_ref, gmix_ref, win_ref, gq_ref, wq_ref, gkv_ref, wkv_ref, tab_ref,
              gv_ref, ws_ref, bs_ref, gog_ref,
              q_ref, k_ref, v_ref, og_ref, og_scr):
    scale = 1.0 / math.sqrt(NOPE + ROPE)
    hm = _rms(x_ref[...], gmix_ref[...]).astype(BF16)
    z = _dot(hm, win_ref[...])
    cq = _rms(z[:, C_Q0:C_Q0 + Q_LORA], gq_ref[...]).astype(BF16)
    ckv = _rms(z[:, C_KV0:C_KV0 + KV_LORA], gkv_ref[...]).astype(BF16)
    q = _dot(cq, wq_ref[...])
    kv = _dot(ckv, wkv_ref[...])
    cos_t = tab_ref[:, :LANES]
    sin_t = tab_ref[:, LANES:]
    kr = (z[:, C_KR0:C_KR0 + LANES] * cos_t
          + z[:, C_KR0 + LANES:C_KR0 + 2 * LANES] * sin_t).astype(BF16)
    for h in range(HEADS):
        c = h * QK_PAD
        q_ref[:, c:c + LANES] = (q[:, c:c + LANES] * scale).astype(BF16)
        p0 = HEADS * QK_PAD + h * LANES
        qr = q[:, c + LANES:c + QK_PAD] * cos_t + q[:, p0:p0 + LANES] * sin_t
        q_ref[:, c + LANES:c + QK_PAD] = (qr * scale).astype(BF16)
        k_ref[:, c:c + LANES] = kv[:, c:c + LANES].astype(BF16)
        k_ref[:, c + LANES:c + QK_PAD] = kr
        v_ref[:, h * V_DIM:(h + 1) * V_DIM] = kv[:, c + LANES:c + QK_PAD].astype(BF16)

    tm = x_ref.shape[0]
    for g in range(GROUPS):
        cg = g * GROUP_DIM
        vg = jax.nn.gelu(z[:, C_V0 + cg:C_V0 + cg + GROUP_DIM])
        vn = _rms(vg, gv_ref[:, cg:cg + GROUP_DIM]).astype(BF16)
        ug = jax.nn.gelu(z[:, C_U0 + cg:C_U0 + cg + GROUP_DIM])
        bias = bs_ref[:, g:g + 1]
        w = ws_ref[g]
        for n in range(tm // CHUNK):
            r = slice(n * CHUNK, (n + 1) * CHUNK)
            og_scr[r, cg:cg + GROUP_DIM] = ug[r] * (_dot(w, vn[r]) + bias)
    og_ref[...] = _rms(og_scr[...], gog_ref[...]).astype(BF16)


def _mix(x1, seq, g_mix, w_in, g_q, w_q, g_kv, w_kv, tab, g_v, w_s, b_t, g_og):
    t = x1.shape[0]
    tm = MIX_TM
    blocks_per_seq = seq // tm
    row = lambda w: pl.BlockSpec((tm, w), lambda i: (i, 0))
    return pl.pallas_call(
        _mix_body,
        grid=(t // tm,),
        in_specs=[row(D_MODEL), _resident((1, D_MODEL)), _resident((D_MODEL, IN_COLS)),
                  _resident((1, Q_LORA)), _resident((Q_LORA, Q_COLS)),
                  _resident((1, KV_LORA)), _resident((KV_LORA, HEADS * QK_PAD)),
                  pl.BlockSpec((tm, 2 * LANES), lambda i: (i % blocks_per_seq, 0)),
                  _resident((1, GMLP_WIDTH)), _resident((GROUPS, CHUNK, CHUNK)),
                  _resident((CHUNK, GROUPS)), _resident((1, GMLP_WIDTH))],
        out_specs=[row(HEADS * QK_PAD), row(HEADS * QK_PAD), row(MLA_WIDTH), row(GMLP_WIDTH)],
        out_shape=[jax.ShapeDtypeStruct((t, HEADS * QK_PAD), BF16),
                   jax.ShapeDtypeStruct((t, HEADS * QK_PAD), BF16),
                   jax.ShapeDtypeStruct((t, MLA_WIDTH), BF16),
                   jax.ShapeDtypeStruct((t, GMLP_WIDTH), BF16)],
        scratch_shapes=[pltpu.VMEM((tm, GMLP_WIDTH), F32)],
        compiler_params=_params("parallel"),
        name="mix",
    )(x1, g_mix, w_in, g_q, w_q, g_kv, w_kv, tab, g_v, w_s, b_t, g_og)


def _attn_body(q_ref, k_ref, v_ref, o_ref):
    s = lax.dot_general(q_ref[...], k_ref[...], (((1,), (1,)), ((), ())),
                        preferred_element_type=F32)
    p = jnp.exp(s - jnp.max(s, axis=-1, keepdims=True))
    denom = jnp.sum(p, axis=-1, keepdims=True)
    o_ref[...] = _dot(p.astype(BF16), v_ref[...]) / denom


def _attn(q, k, v, batch, seq):
    tq = 256 if seq > 2048 else 512
    nq = seq // tq
    return pl.pallas_call(
        _attn_body,
        grid=(batch, HEADS, nq),
        in_specs=[pl.BlockSpec((tq, QK_PAD), lambda b, h, i: (b * nq + i, h)),
                  pl.BlockSpec((seq, QK_PAD), lambda b, h, i: (b, h)),
                  pl.BlockSpec((seq, V_DIM), lambda b, h, i: (b, h))],
        out_specs=pl.BlockSpec((tq, V_DIM), lambda b, h, i: (b * nq + i, h)),
        out_shape=jax.ShapeDtypeStruct((batch * seq, MLA_WIDTH), F32),
        compiler_params=_params("parallel", "parallel", "arbitrary"),
        name="attn",
    )(q, k, v)


def _merge_body(x_ref, oa_ref, og_ref, goa_ref, wo_ref, o_ref):
    on = _rms(oa_ref[...], goa_ref[...]).astype(BF16)
    o_ref[...] = (x_ref[...] + _dot(on, wo_ref[:MLA_WIDTH, :])
                  + _dot(og_ref[...], wo_ref[MLA_WIDTH:, :]))


def _merge(x1, oa, og, g_oa, w_out):
    t = x1.shape[0]
    tm = MERGE_TM
    row = lambda w: pl.BlockSpec((tm, w), lambda i: (i, 0))
    return pl.pallas_call(
        _merge_body,
        grid=(t // tm,),
        in_specs=[row(D_MODEL), row(MLA_WIDTH), row(GMLP_WIDTH),
                  _resident((1, MLA_WIDTH)), _resident((MLA_WIDTH + GMLP_WIDTH, D_MODEL))],
        out_specs=row(D_MODEL),
        out_shape=jax.ShapeDtypeStruct((t, D_MODEL), F32),
        compiler_params=_params("parallel"),
        name="merge",
    )(x1, oa, og, g_oa, w_out)


def _rope_table(seq):
    inv = 1.0 / (ROPE_THETA ** (jnp.arange(0, ROPE, 2, dtype=F32) / ROPE))
    ang = jnp.arange(seq, dtype=F32)[:, None] * inv[None, :]
    zero = jnp.zeros((seq, LANES - ROPE), F32)
    cos, sin = jnp.cos(ang), jnp.sin(ang)
    return jnp.concatenate([cos, cos, zero, sin, sin, zero], axis=1)


def _relay_w_in(w_in):
    r1 = w_in[:, C_KR0:C_KR0 + HALF_ROPE]
    r2 = w_in[:, C_KR0 + HALF_ROPE:C_KR0 + ROPE]
    pad = jnp.zeros((D_MODEL, LANES - ROPE), w_in.dtype)
    return jnp.concatenate(
        [w_in[:, :C_KR0], r1, r2, pad, -r2, r1, pad, w_in[:, C_KR0 + ROPE:]], axis=1).astype(BF16)


def _relay_w_q(w_q):
    w = w_q.reshape(Q_LORA, HEADS, NOPE + ROPE)
    nope, r1, r2 = w[..., :NOPE], w[..., NOPE:NOPE + HALF_ROPE], w[..., NOPE + HALF_ROPE:]
    pad = jnp.zeros((Q_LORA, HEADS, LANES - ROPE), w_q.dtype)
    main = jnp.concatenate([nope, r1, r2, pad], axis=-1).reshape(Q_LORA, HEADS * QK_PAD)
    partner = jnp.concatenate([-r2, r1, pad], axis=-1).reshape(Q_LORA, HEADS * LANES)
    return jnp.concatenate([main, partner], axis=1).astype(BF16)


def _layer(x, seq, w):
    batch = x.shape[0]
    x = x.reshape(batch * seq, D_MODEL)
    x1 = _ffn(x, w["g_ffn1"], w["w1_gate"], w["w1_up"], w["w1_down"])
    q, k, v, og = _mix(x1, seq, w["g_mix"], w["w_in"], w["g_q"], w["w_q"], w["g_kv"], w["w_kv"],
                       _rope_table(seq), w["g_v"], w["w_s"], w["b_t"], w["g_og"])
    oa = _attn(q, k, v, batch, seq)
    x2 = _merge(x1, oa, og, w["g_oa"], w["w_out"])
    y = _ffn(x2, w["g_ffn2"], w["w2_gate"], w["w2_up"], w["w2_down"], w["g_final"])
    return y.reshape(batch, seq, D_MODEL)


def kernel(x_prompt, x_sample, g_ffn1, w1_gate, w1_up, w1_down, g_mix, w_in, g_q, w_q_b, g_kv,
           w_kv_b, g_v, w_s, b_s, g_out_attn, g_out_gmlp, w_out, g_ffn2, w2_gate, w2_up,
           w2_down, g_final):
    row = lambda g: g.reshape(1, -1)
    w = dict(
        g_ffn1=row(g_ffn1[0]), w1_gate=w1_gate[0].astype(BF16), w1_up=w1_up[0].astype(BF16),
        w1_down=w1_down[0].astype(BF16),
        g_mix=row(g_mix[0]), w_in=_relay_w_in(w_in[0]),
        g_q=row(g_q[0]), w_q=_relay_w_q(w_q_b[0]),
        g_kv=row(g_kv[0]), w_kv=w_kv_b[0].astype(BF16),
        g_v=row(g_v[0]), w_s=w_s[0].astype(BF16), b_t=jnp.transpose(b_s[0]),
        g_og=row(g_out_gmlp[0]), g_oa=row(g_out_attn[0]), w_out=w_out[0].astype(BF16),
        g_ffn2=row(g_ffn2[0]), w2_gate=w2_gate[0].astype(BF16), w2_up=w2_up[0].astype(BF16),
        w2_down=w2_down[0].astype(BF16), g_final=row(g_final),
    )
    y_prompt = _layer(x_prompt, x_prompt.shape[1], w)
    y_sample = _layer(x_sample, x_sample.shape[1], w)
    return (y_prompt, y_sample)
```

```python
import functools
import math

import jax
import jax.numpy as jnp
from jax import lax
from jax.experimental import pallas as pl
from jax.experimental.pallas import tpu as pltpu

F32 = jnp.float32
BF16 = jnp.bfloat16

D_MODEL = 2048
HEADS = 8
NOPE = 128
ROPE = 64
HALF_ROPE = ROPE // 2
V_DIM = 128
Q_LORA = 512
KV_LORA = 512
GROUPS = 8
GROUP_DIM = 128
GMLP_WIDTH = GROUPS * GROUP_DIM
MLA_WIDTH = HEADS * V_DIM
CHUNK = 128
D_FF = 5632
ROPE_THETA = 10000.0
EPS = 1e-6

LANES = 128
QK_PAD = 2 * LANES
C_Q0, C_KV0 = 0, Q_LORA
C_KR0 = Q_LORA + KV_LORA
C_U0 = C_KR0 + 2 * LANES
C_V0 = C_U0 + GMLP_WIDTH
IN_COLS = C_V0 + GMLP_WIDTH
Q_COLS = HEADS * QK_PAD + HEADS * LANES

VMEM_LIMIT_BYTES = 60000 * 1024

FFN_TM, FFN_TF = 512, 512
MIX_TM = 256
MERGE_TM = 512
ATTN_BLOCKING = {8192: (256, 4), 2048: (512, 4)}


def _rms(x, g):
    return x * lax.rsqrt(jnp.mean(x * x, axis=-1, keepdims=True) + EPS) * g


def _dot(a, b):
    return jnp.dot(a, b, preferred_element_type=F32)


def _params(*sem):
    return pltpu.CompilerParams(dimension_semantics=sem, vmem_limit_bytes=VMEM_LIMIT_BYTES)


def _resident(shape):
    return pl.BlockSpec(shape, lambda *_: (0,) * len(shape), pipeline_mode=pl.Buffered(1))


def _ffn_body(final_norm, x_ref, g_ref, wg_ref, wu_ref, wd_ref, *rest):
    if final_norm:
        gf_ref, o_ref, h_ref = rest
    else:
        o_ref, h_ref = rest
    j = pl.program_id(1)

    @pl.when(j == 0)
    def _():
        x = x_ref[...]
        h_ref[...] = _rms(x, g_ref[...]).astype(BF16)
        o_ref[...] = x

    h = h_ref[...]
    gate = _dot(h, wg_ref[...])
    up = _dot(h, wu_ref[...])
    act = (0.5 * gate * jax.nn.sigmoid(gate) * up).astype(BF16)
    o_ref[...] += _dot(act, wd_ref[...])

    if final_norm:
        @pl.when(j == pl.num_programs(1) - 1)
        def _():
            o_ref[...] = _rms(o_ref[...], gf_ref[...])


def _ffn(x, g, wg, wu, wd, g_final=None):
    t = x.shape[0]
    final_norm = g_final is not None
    row = pl.BlockSpec((FFN_TM, D_MODEL), lambda i, j: (i, 0))
    gain = pl.BlockSpec((1, D_MODEL), lambda i, j: (0, 0))
    in_specs = [row, gain,
                pl.BlockSpec((D_MODEL, FFN_TF), lambda i, j: (0, j)),
                pl.BlockSpec((D_MODEL, FFN_TF), lambda i, j: (0, j)),
                pl.BlockSpec((FFN_TF, D_MODEL), lambda i, j: (j, 0))]
    args = [x, g, wg, wu, wd]
    if final_norm:
        in_specs.append(gain)
        args.append(g_final)
    return pl.pallas_call(
        functools.partial(_ffn_body, final_norm),
        grid=(t // FFN_TM, D_FF // FFN_TF),
        in_specs=in_specs,
        out_specs=row,
        out_shape=jax.ShapeDtypeStruct((t, D_MODEL), F32),
        scratch_shapes=[pltpu.VMEM((FFN_TM, D_MODEL), BF16)],
        compiler_params=_params("parallel", "arbitrary"),
        name="ffn_final" if final_norm else "ffn",
    )(*args)


def _mix_body(x_ref, gmix_ref, win_ref, gq_ref, wq_ref, gkv_ref, wkv_ref, tab_ref,
              gv_ref, ws_ref, bs_ref, gog_ref,
              q_ref, k_ref, v_ref, og_ref, og_scr):
    scale = math.log2(math.e) / math.sqrt(NOPE + ROPE)
    hm =_rms(x_ref[...], gmix_ref[...]).astype(BF16)
    z = _dot(hm, win_ref[...])
    cq = _rms(z[:, C_Q0:C_Q0 + Q_LORA], gq_ref[...]).astype(BF16)
    ckv = _rms(z[:, C_KV0:C_KV0 + KV_LORA], gkv_ref[...]).astype(BF16)
    q = _dot(cq, wq_ref[...])
    kv = _dot(ckv, wkv_ref[...])
    cos_t = tab_ref[:, :LANES]
    sin_t = tab_ref[:, LANES:]
    kr = (z[:, C_KR0:C_KR0 + LANES] * cos_t
          + z[:, C_KR0 + LANES:C_KR0 + 2 * LANES] * sin_t).astype(BF16)
    for h in range(HEADS):
        c = h * QK_PAD
        q_ref[:, c:c + LANES] = (q[:, c:c + LANES] * scale).astype(BF16)
        p0 = HEADS * QK_PAD + h * LANES
        qr = q[:, c + LANES:c + QK_PAD] * cos_t + q[:, p0:p0 + LANES] * sin_t
        q_ref[:, c + LANES:c + QK_PAD] = (qr * scale).astype(BF16)
        k_ref[:, c:c + LANES] = kv[:, c:c + LANES].astype(BF16)
        k_ref[:, c + LANES:c + QK_PAD] = kr
        v_ref[:, h * V_DIM:(h + 1) * V_DIM] = kv[:, c + LANES:c + QK_PAD].astype(BF16)

    tm = x_ref.shape[0]
    for g in range(GROUPS):
        cg = g * GROUP_DIM
        vg = jax.nn.gelu(z[:, C_V0 + cg:C_V0 + cg + GROUP_DIM])
        vn = _rms(vg, gv_ref[:, cg:cg + GROUP_DIM]).astype(BF16)
        ug = jax.nn.gelu(z[:, C_U0 + cg:C_U0 + cg + GROUP_DIM])
        bias = bs_ref[:, g:g + 1]
        w = ws_ref[g]
        for n in range(tm // CHUNK):
            r = slice(n * CHUNK, (n + 1) * CHUNK)
            og_scr[r, cg:cg + GROUP_DIM] = ug[r] * (_dot(w, vn[r]) + bias)
    og_ref[...] = _rms(og_scr[...], gog_ref[...]).astype(BF16)


def _mix(x1, seq, g_mix, w_in, g_q, w_q, g_kv, w_kv, tab, g_v, w_s, b_t, g_og):
    t = x1.shape[0]
    tm = MIX_TM
    blocks_per_seq = seq // tm
    row = lambda w: pl.BlockSpec((tm, w), lambda i: (i, 0))
    return pl.pallas_call(
        _mix_body,
        grid=(t // tm,),
        in_specs=[row(D_MODEL), _resident((1, D_MODEL)), _resident((D_MODEL, IN_COLS)),
                  _resident((1, Q_LORA)), _resident((Q_LORA, Q_COLS)),
                  _resident((1, KV_LORA)), _resident((KV_LORA, HEADS * QK_PAD)),
                  pl.BlockSpec((tm, 2 * LANES), lambda i: (i % blocks_per_seq, 0)),
                  _resident((1, GMLP_WIDTH)), _resident((GROUPS, CHUNK, CHUNK)),
                  _resident((CHUNK, GROUPS)), _resident((1, GMLP_WIDTH))],
        out_specs=[row(HEADS * QK_PAD), row(HEADS * QK_PAD), row(MLA_WIDTH), row(GMLP_WIDTH)],
        out_shape=[jax.ShapeDtypeStruct((t, HEADS * QK_PAD), BF16),
                   jax.ShapeDtypeStruct((t, HEADS * QK_PAD), BF16),
                   jax.ShapeDtypeStruct((t, MLA_WIDTH), BF16),
                   jax.ShapeDtypeStruct((t, GMLP_WIDTH), BF16)],
        scratch_shapes=[pltpu.VMEM((tm, GMLP_WIDTH), F32)],
        compiler_params=_params("parallel"),
        name="mix",
    )(x1, g_mix, w_in, g_q, w_q, g_kv, w_kv, tab, g_v, w_s, b_t, g_og)


def _attn_body(nsub, q_ref, k_ref, v_ref, o_ref):
    tq = q_ref.shape[0] // nsub

    def scores(i):
        return lax.dot_general(q_ref[i * tq:(i + 1) * tq, :], k_ref[...],
                               (((1,), (1,)), ((), ())), preferred_element_type=F32)

    def finish(i, s):
        p = jnp.exp2(s - jnp.max(s, axis=-1, keepdims=True))
        denom = jnp.sum(p, axis=-1, keepdims=True)
        o_ref[i * tq:(i + 1) * tq, :] = _dot(p.astype(BF16), v_ref[...]) / denom

    s_cur = scores(0)
    for i in range(nsub):
        s_next = scores(i + 1) if i + 1 < nsub else None
        finish(i, s_cur)
        s_cur = s_next


def _attn(q, k, v, batch, seq):
    tq, nsub = ATTN_BLOCKING[seq]
    rows = tq * nsub
    nq = seq // rows
    return pl.pallas_call(
        functools.partial(_attn_body, nsub),
        grid=(batch, HEADS, nq),
        in_specs=[pl.BlockSpec((rows, QK_PAD), lambda b, h, i: (b * nq + i, h)),
                  pl.BlockSpec((seq, QK_PAD), lambda b, h, i: (b, h)),
                  pl.BlockSpec((seq, V_DIM), lambda b, h, i: (b, h))],
        out_specs=pl.BlockSpec((rows, V_DIM), lambda b, h, i: (b * nq + i, h)),
        out_shape=jax.ShapeDtypeStruct((batch * seq, MLA_WIDTH), F32),
        compiler_params=_params("parallel", "parallel", "arbitrary"),
        name="attn",
    )(q, k, v)


def _merge_body(x_ref, oa_ref, og_ref, goa_ref, wo_ref, o_ref):
    on = _rms(oa_ref[...], goa_ref[...]).astype(BF16)
    o_ref[...] = (x_ref[...] + _dot(on, wo_ref[:MLA_WIDTH, :])
                  + _dot(og_ref[...], wo_ref[MLA_WIDTH:, :]))


def _merge(x1, oa, og, g_oa, w_out):
    t = x1.shape[0]
    tm = MERGE_TM
    row = lambda w: pl.BlockSpec((tm, w), lambda i: (i, 0))
    return pl.pallas_call(
        _merge_body,
        grid=(t // tm,),
        in_specs=[row(D_MODEL), row(MLA_WIDTH), row(GMLP_WIDTH),
                  _resident((1, MLA_WIDTH)), _resident((MLA_WIDTH + GMLP_WIDTH, D_MODEL))],
        out_specs=row(D_MODEL),
        out_shape=jax.ShapeDtypeStruct((t, D_MODEL), F32),
        compiler_params=_params("parallel"),
        name="merge",
    )(x1, oa, og, g_oa, w_out)


def _rope_table(seq):
    inv = 1.0 / (ROPE_THETA ** (jnp.arange(0, ROPE, 2, dtype=F32) / ROPE))
    ang = jnp.arange(seq, dtype=F32)[:, None] * inv[None, :]
    zero = jnp.zeros((seq, LANES - ROPE), F32)
    cos, sin = jnp.cos(ang), jnp.sin(ang)
    return jnp.concatenate([cos, cos, zero, sin, sin, zero], axis=1)


def _relay_w_in(w_in):
    r1 = w_in[:, C_KR0:C_KR0 + HALF_ROPE]
    r2 = w_in[:, C_KR0 + HALF_ROPE:C_KR0 + ROPE]
    pad = jnp.zeros((D_MODEL, LANES - ROPE), w_in.dtype)
    return jnp.concatenate(
        [w_in[:, :C_KR0], r1, r2, pad, -r2, r1, pad, w_in[:, C_KR0 + ROPE:]], axis=1).astype(BF16)


def _relay_w_q(w_q):
    w = w_q.reshape(Q_LORA, HEADS, NOPE + ROPE)
    nope, r1, r2 = w[..., :NOPE], w[..., NOPE:NOPE + HALF_ROPE], w[..., NOPE + HALF_ROPE:]
    pad = jnp.zeros((Q_LORA, HEADS, LANES - ROPE), w_q.dtype)
    main = jnp.concatenate([nope, r1, r2, pad], axis=-1).reshape(Q_LORA, HEADS * QK_PAD)
    partner = jnp.concatenate([-r2, r1, pad], axis=-1).reshape(Q_LORA, HEADS * LANES)
    return jnp.concatenate([main, partner], axis=1).astype(BF16)


def _layer(x, seq, w):
    batch = x.shape[0]
    x = x.reshape(batch * seq, D_MODEL)
    x1 = _ffn(x, w["g_ffn1"], w["w1_gate"], w["w1_up"], w["w1_down"])
    q, k, v, og = _mix(x1, seq, w["g_mix"], w["w_in"], w["g_q"], w["w_q"], w["g_kv"], w["w_kv"],
                       _rope_table(seq), w["g_v"], w["w_s"], w["b_t"], w["g_og"])
    oa = _attn(q, k, v, batch, seq)
    x2 = _merge(x1, oa, og, w["g_oa"], w["w_out"])
    y = _ffn(x2, w["g_ffn2"], w["w2_gate"], w["w2_up"], w["w2_down"], w["g_final"])
    return y.reshape(batch, seq, D_MODEL)


def kernel(x_prompt, x_sample, g_ffn1, w1_gate, w1_up, w1_down, g_mix, w_in, g_q, w_q_b, g_kv,
           w_kv_b, g_v, w_s, b_s, g_out_attn, g_out_gmlp, w_out, g_ffn2, w2_gate, w2_up,
           w2_down, g_final):
    row = lambda g: g.reshape(1, -1)
    w = dict(
        g_ffn1=row(g_ffn1[0]), w1_gate=w1_gate[0].astype(BF16), w1_up=w1_up[0].astype(BF16),
        w1_down=w1_down[0].astype(BF16),
        g_mix=row(g_mix[0]), w_in=_relay_w_in(w_in[0]),
        g_q=row(g_q[0]), w_q=_relay_w_q(w_q_b[0]),
        g_kv=row(g_kv[0]), w_kv=w_kv_b[0].astype(BF16),
        g_v=row(g_v[0]), w_s=w_s[0].astype(BF16), b_t=jnp.transpose(b_s[0]),
        g_og=row(g_out_gmlp[0]), g_oa=row(g_out_attn[0]), w_out=w_out[0].astype(BF16),
        g_ffn2=row(g_ffn2[0]), w2_gate=w2_gate[0].astype(BF16), w2_up=w2_up[0].astype(BF16),
        w2_down=w2_down[0].astype(BF16), g_final=row(g_final),
    )
    y_prompt = _layer(x_prompt, x_prompt.shape[1], w)
    y_sample = _layer(x_sample, x_sample.shape[1], w)
    return (y_prompt, y_sample)
```

```python
import functools
import math

import jax
import jax.numpy as jnp
from jax import lax
from jax.experimental import pallas as pl
from jax.experimental.pallas import tpu as pltpu

F32 = jnp.float32
BF16 = jnp.bfloat16

D_MODEL = 2048
HEADS = 8
NOPE = 128
ROPE = 64
HALF_ROPE = ROPE // 2
V_DIM = 128
Q_LORA = 512
KV_LORA = 512
GROUPS = 8
GROUP_DIM = 128
GMLP_WIDTH = GROUPS * GROUP_DIM
MLA_WIDTH = HEADS * V_DIM
CHUNK = 128
D_FF = 5632
ROPE_THETA = 10000.0
EPS = 1e-6

LANES = 128
SUBLANES = 8
ATTN_TK = 1024
QK_PAD = 2 * LANES
C_Q0, C_KV0 = 0, Q_LORA
C_KR0 = Q_LORA + KV_LORA
C_U0 = C_KR0 + 2 * LANES
C_V0 = C_U0 + GMLP_WIDTH
IN_COLS = C_V0 + GMLP_WIDTH
Q_COLS = HEADS * QK_PAD + HEADS * LANES

VMEM_LIMIT_BYTES = 60000 * 1024

FFN_TM, FFN_TF = 1024, 512
MIX_TM = 256
MERGE_TM = 512
ATTN_BLOCKING = {8192: (256, 4), 2048: (512, 4)}


def _rms(x, g):
    return x * lax.rsqrt(jnp.mean(x * x, axis=-1, keepdims=True) + EPS) * g


def _dot(a, b):
    return jnp.dot(a, b, preferred_element_type=F32)


def _params(*sem):
    return pltpu.CompilerParams(dimension_semantics=sem, vmem_limit_bytes=VMEM_LIMIT_BYTES)


def _resident(shape):
    return pl.BlockSpec(shape, lambda *_: (0,) * len(shape), pipeline_mode=pl.Buffered(1))


def _ffn_body(final_norm, x_ref, g_ref, wg_ref, wu_ref, wd_ref, *rest):
    if final_norm:
        gf_ref, o_ref, h_ref = rest
    else:
        o_ref, h_ref = rest
    j = pl.program_id(1)

    @pl.when(j == 0)
    def _():
        x = x_ref[...]
        h_ref[...] = _rms(x, g_ref[...]).astype(BF16)
        o_ref[...] = x

    h = h_ref[...]
    gate = _dot(h, wg_ref[...])
    up = _dot(h, wu_ref[...])
    act = (0.5 * gate * jax.nn.sigmoid(gate) * up).astype(BF16)
    o_ref[...] += _dot(act, wd_ref[...])

    if final_norm:
        @pl.when(j == pl.num_programs(1) - 1)
        def _():
            o_ref[...] = _rms(o_ref[...], gf_ref[...])


def _ffn(x, g, wg, wu, wd, g_final=None):
    t = x.shape[0]
    final_norm = g_final is not None
    row = pl.BlockSpec((FFN_TM, D_MODEL), lambda i, j: (i, 0))
    gain = pl.BlockSpec((1, D_MODEL), lambda i, j: (0, 0))
    in_specs = [row, gain,
                pl.BlockSpec((D_MODEL, FFN_TF), lambda i, j: (0, j)),
                pl.BlockSpec((D_MODEL, FFN_TF), lambda i, j: (0, j)),
                pl.BlockSpec((FFN_TF, D_MODEL), lambda i, j: (j, 0))]
    args = [x, g, wg, wu, wd]
    if final_norm:
        in_specs.append(gain)
        args.append(g_final)
    return pl.pallas_call(
        functools.partial(_ffn_body, final_norm),
        grid=(t // FFN_TM, D_FF // FFN_TF),
        in_specs=in_specs,
        out_specs=row,
        out_shape=jax.ShapeDtypeStruct((t, D_MODEL), F32),
        scratch_shapes=[pltpu.VMEM((FFN_TM, D_MODEL), BF16)],
        compiler_params=_params("parallel", "arbitrary"),
        name="ffn_final" if final_norm else "ffn",
    )(*args)


def _mix_body(x_ref, gmix_ref, win_ref, gq_ref, wq_ref, gkv_ref, wkv_ref, tab_ref,
              gv_ref, ws_ref, bs_ref, gog_ref,
              q_ref, k_ref, vt_ref, og_ref, og_scr):
    scale = math.log2(math.e) / math.sqrt(NOPE + ROPE)
    hm =_rms(x_ref[...], gmix_ref[...]).astype(BF16)
    z = _dot(hm, win_ref[...])
    cq = _rms(z[:, C_Q0:C_Q0 + Q_LORA], gq_ref[...]).astype(BF16)
    ckv = _rms(z[:, C_KV0:C_KV0 + KV_LORA], gkv_ref[...]).astype(BF16)
    q = _dot(cq, wq_ref[...])
    kv = _dot(ckv, wkv_ref[...])
    cos_t = tab_ref[:, :LANES]
    sin_t = tab_ref[:, LANES:]
    kr = (z[:, C_KR0:C_KR0 + LANES] * cos_t
          + z[:, C_KR0 + LANES:C_KR0 + 2 * LANES] * sin_t).astype(BF16)
    for h in range(HEADS):
        c = h * QK_PAD
        q_ref[:, c:c + LANES] = (q[:, c:c + LANES] * scale).astype(BF16)
        p0 = HEADS * QK_PAD + h * LANES
        qr = q[:, c + LANES:c + QK_PAD] * cos_t + q[:, p0:p0 + LANES] * sin_t
        q_ref[:, c + LANES:c + QK_PAD] = (qr * scale).astype(BF16)
        k_ref[:, c:c + LANES] = kv[:, c:c + LANES].astype(BF16)
        k_ref[:, c + LANES:c + QK_PAD] = kr
        vt_ref[h] = kv[:, c + LANES:c + QK_PAD].T.astype(BF16)

    tm = x_ref.shape[0]
    for g in range(GROUPS):
        cg = g * GROUP_DIM
        vg = jax.nn.gelu(z[:, C_V0 + cg:C_V0 + cg + GROUP_DIM])
        vn = _rms(vg, gv_ref[:, cg:cg + GROUP_DIM]).astype(BF16)
        ug = jax.nn.gelu(z[:, C_U0 + cg:C_U0 + cg + GROUP_DIM])
        bias = bs_ref[:, g:g + 1]
        w = ws_ref[g]
        for n in range(tm // CHUNK):
            r = slice(n * CHUNK, (n + 1) * CHUNK)
            og_scr[r, cg:cg + GROUP_DIM] = ug[r] * (_dot(w, vn[r]) + bias)
    og_ref[...] = _rms(og_scr[...], gog_ref[...]).astype(BF16)


def _mix(x1, seq, g_mix, w_in, g_q, w_q, g_kv, w_kv, tab, g_v, w_s, b_t, g_og):
    t = x1.shape[0]
    tm = MIX_TM
    blocks_per_seq = seq // tm
    row = lambda w: pl.BlockSpec((tm, w), lambda i: (i, 0))
    return pl.pallas_call(
        _mix_body,
        grid=(t // tm,),
        in_specs=[row(D_MODEL), _resident((1, D_MODEL)), _resident((D_MODEL, IN_COLS)),
                  _resident((1, Q_LORA)), _resident((Q_LORA, Q_COLS)),
                  _resident((1, KV_LORA)), _resident((KV_LORA, HEADS * QK_PAD)),
                  pl.BlockSpec((tm, 2 * LANES), lambda i: (i % blocks_per_seq, 0)),
                  _resident((1, GMLP_WIDTH)), _resident((GROUPS, CHUNK, CHUNK)),
                  _resident((CHUNK, GROUPS)), _resident((1, GMLP_WIDTH))],
        out_specs=[row(HEADS * QK_PAD), row(HEADS * QK_PAD),
                   pl.BlockSpec((HEADS, V_DIM, tm), lambda i: (0, 0, i)), row(GMLP_WIDTH)],
        out_shape=[jax.ShapeDtypeStruct((t, HEADS * QK_PAD), BF16),
                   jax.ShapeDtypeStruct((t, HEADS * QK_PAD), BF16),
                   jax.ShapeDtypeStruct((HEADS, V_DIM, t), BF16),
                   jax.ShapeDtypeStruct((t, GMLP_WIDTH), BF16)],
        scratch_shapes=[pltpu.VMEM((tm, GMLP_WIDTH), F32)],
        compiler_params=_params("parallel"),
        name="mix",
    )(x1, g_mix, w_in, g_q, w_q, g_kv, w_kv, tab, g_v, w_s, b_t, g_og)


def _attn_body(nsub, q_ref, k_ref, vt_ref, o_ref, s_scr):
    tq = q_ref.shape[0] // nsub
    seq = k_ref.shape[0]
    nchunk = seq // ATTN_TK

    def fold(x, op):
        return functools.reduce(op, [x[r:r + SUBLANES] for r in range(0, ATTN_TK, SUBLANES)])

    def score_chunk(i, c):
        rows = slice(c * ATTN_TK, (c + 1) * ATTN_TK)
        s = lax.dot_general(k_ref[rows, :], q_ref[i * tq:(i + 1) * tq, :],
                            (((1,), (1,)), ((), ())), preferred_element_type=F32)
        s_scr[i % 2, rows, :] = s
        return fold(s, jnp.maximum)

    mx = functools.reduce(jnp.maximum, [score_chunk(0, c) for c in range(nchunk)])
    for i in range(nsub):
        m = jnp.max(mx, axis=0, keepdims=True)
        mx = lsum = acc = None
        for c in range(nchunk):
            rows = slice(c * ATTN_TK, (c + 1) * ATTN_TK)
            if i + 1 < nsub:
                cm = score_chunk(i + 1, c)
                mx = cm if mx is None else jnp.maximum(mx, cm)
            p = jnp.exp2(s_scr[i % 2, rows, :] - m)
            ps = fold(p, jnp.add)
            pv = _dot(vt_ref[:, rows], p.astype(BF16))
            lsum = ps if lsum is None else lsum + ps
            acc = pv if acc is None else acc + pv
        o_t = acc / jnp.sum(lsum, axis=0, keepdims=True)
        o_ref[i * tq:(i + 1) * tq, :] = o_t.T


def _attn(q, k, v, batch, seq):
    tq, nsub = ATTN_BLOCKING[seq]
    rows = tq * nsub
    nq = seq // rows
    return pl.pallas_call(
        functools.partial(_attn_body, nsub),
        grid=(batch, HEADS, nq),
        in_specs=[pl.BlockSpec((rows, QK_PAD), lambda b, h, i: (b * nq + i, h)),
                  pl.BlockSpec((seq, QK_PAD), lambda b, h, i: (b, h)),
                  pl.BlockSpec((None, V_DIM, seq), lambda b, h, i: (h, 0, b))],
        out_specs=pl.BlockSpec((rows, V_DIM), lambda b, h, i: (b * nq + i, h)),
        out_shape=jax.ShapeDtypeStruct((batch * seq, MLA_WIDTH), F32),
        scratch_shapes=[pltpu.VMEM((2, seq, tq), F32)],
        compiler_params=_params("parallel", "parallel", "arbitrary"),
        name="attn",
    )(q, k, v)


def _merge_body(x_ref, oa_ref, og_ref, goa_ref, wo_ref, o_ref):
    on = _rms(oa_ref[...], goa_ref[...]).astype(BF16)
    o_ref[...] = (x_ref[...] + _dot(on, wo_ref[:MLA_WIDTH, :])
                  + _dot(og_ref[...], wo_ref[MLA_WIDTH:, :]))


def _merge(x1, oa, og, g_oa, w_out):
    t = x1.shape[0]
    tm = MERGE_TM
    row = lambda w: pl.BlockSpec((tm, w), lambda i: (i, 0))
    return pl.pallas_call(
        _merge_body,
        grid=(t // tm,),
        in_specs=[row(D_MODEL), row(MLA_WIDTH), row(GMLP_WIDTH),
                  _resident((1, MLA_WIDTH)), _resident((MLA_WIDTH + GMLP_WIDTH, D_MODEL))],
        out_specs=row(D_MODEL),
        out_shape=jax.ShapeDtypeStruct((t, D_MODEL), F32),
        compiler_params=_params("parallel"),
        name="merge",
    )(x1, oa, og, g_oa, w_out)


def _rope_table(seq):
    inv = 1.0 / (ROPE_THETA ** (jnp.arange(0, ROPE, 2, dtype=F32) / ROPE))
    ang = jnp.arange(seq, dtype=F32)[:, None] * inv[None, :]
    zero = jnp.zeros((seq, LANES - ROPE), F32)
    cos, sin = jnp.cos(ang), jnp.sin(ang)
    return jnp.concatenate([cos, cos, zero, sin, sin, zero], axis=1)


def _relay_w_in(w_in):
    r1 = w_in[:, C_KR0:C_KR0 + HALF_ROPE]
    r2 = w_in[:, C_KR0 + HALF_ROPE:C_KR0 + ROPE]
    pad = jnp.zeros((D_MODEL, LANES - ROPE), w_in.dtype)
    return jnp.concatenate(
        [w_in[:, :C_KR0], r1, r2, pad, -r2, r1, pad, w_in[:, C_KR0 + ROPE:]], axis=1).astype(BF16)


def _relay_w_q(w_q):
    w = w_q.reshape(Q_LORA, HEADS, NOPE + ROPE)
    nope, r1, r2 = w[..., :NOPE], w[..., NOPE:NOPE + HALF_ROPE], w[..., NOPE + HALF_ROPE:]
    pad = jnp.zeros((Q_LORA, HEADS, LANES - ROPE), w_q.dtype)
    main = jnp.concatenate([nope, r1, r2, pad], axis=-1).reshape(Q_LORA, HEADS * QK_PAD)
    partner = jnp.concatenate([-r2, r1, pad], axis=-1).reshape(Q_LORA, HEADS * LANES)
    return jnp.concatenate([main, partner], axis=1).astype(BF16)


def _layer(x, seq, w):
    batch = x.shape[0]
    x = x.reshape(batch * seq, D_MODEL)
    x1 = _ffn(x, w["g_ffn1"], w["w1_gate"], w["w1_up"], w["w1_down"])
    q, k, v, og = _mix(x1, seq, w["g_mix"], w["w_in"], w["g_q"], w["w_q"], w["g_kv"], w["w_kv"],
                       _rope_table(seq), w["g_v"], w["w_s"], w["b_t"], w["g_og"])
    oa = _attn(q, k, v, batch, seq)
    x2 = _merge(x1, oa, og, w["g_oa"], w["w_out"])
    y = _ffn(x2, w["g_ffn2"], w["w2_gate"], w["w2_up"], w["w2_down"], w["g_final"])
    return y.reshape(batch, seq, D_MODEL)


def kernel(x_prompt, x_sample, g_ffn1, w1_gate, w1_up, w1_down, g_mix, w_in, g_q, w_q_b, g_kv,
           w_kv_b, g_v, w_s, b_s, g_out_attn, g_out_gmlp, w_out, g_ffn2, w2_gate, w2_up,
           w2_down, g_final):
    row = lambda g: g.reshape(1, -1)
    w = dict(
        g_ffn1=row(g_ffn1[0]), w1_gate=w1_gate[0].astype(BF16), w1_up=w1_up[0].astype(BF16),
        w1_down=w1_down[0].astype(BF16),
        g_mix=row(g_mix[0]), w_in=_relay_w_in(w_in[0]),
        g_q=row(g_q[0]), w_q=_relay_w_q(w_q_b[0]),
        g_kv=row(g_kv[0]), w_kv=w_kv_b[0].astype(BF16),
        g_v=row(g_v[0]), w_s=w_s[0].astype(BF16), b_t=jnp.transpose(b_s[0]),
        g_og=row(g_out_gmlp[0]), g_oa=row(g_out_attn[0]), w_out=w_out[0].astype(BF16),
        g_ffn2=row(g_ffn2[0]), w2_gate=w2_gate[0].astype(BF16), w2_up=w2_up[0].astype(BF16),
        w2_down=w2_down[0].astype(BF16), g_final=row(g_final),
    )
    y_prompt = _layer(x_prompt, x_prompt.shape[1], w)
    y_sample = _layer(x_sample, x_sample.shape[1], w)
    return (y_prompt, y_sample)
```

```python
import functools
import math

import jax
import jax.numpy as jnp
from jax import lax
from jax.experimental import pallas as pl
from jax.experimental.pallas import tpu as pltpu

F32 = jnp.float32
BF16 = jnp.bfloat16

D_MODEL = 2048
HEADS = 8
NOPE = 128
ROPE = 64
HALF_ROPE = ROPE // 2
V_DIM = 128
Q_LORA = 512
KV_LORA = 512
GROUPS = 8
GROUP_DIM = 128
GMLP_WIDTH = GROUPS * GROUP_DIM
MLA_WIDTH = HEADS * V_DIM
CHUNK = 128
D_FF = 5632
ROPE_THETA = 10000.0
EPS = 1e-6

LANES = 128
SUBLANES = 8
ATTN_AHEAD = 2
QK_PAD = 2 * LANES
PAIRS = HEADS // 2
C_Q0, C_KV0 = 0, Q_LORA
C_KR0 = Q_LORA + KV_LORA
C_U0 = C_KR0 + 2 * LANES
C_V0 = C_U0 + GMLP_WIDTH
IN_COLS = C_V0 + GMLP_WIDTH
Q_NOPE0, Q_ROPE0, Q_PART0 = 0, HEADS * NOPE, HEADS * NOPE + PAIRS * LANES
Q_COLS = Q_PART0 + PAIRS * LANES

VMEM_LIMIT_BYTES = 60000 * 1024

FFN_TM, FFN_TF = 1024, 512
MIX_TM = 512
MERGE_TM = 512
ATTN_BLOCKING = {8192: (256, 4, 1024), 2048: (512, 4, 1024)}


def _rms(x, g):
    return x * lax.rsqrt(jnp.mean(x * x, axis=-1, keepdims=True) + EPS) * g


def _dot(a, b):
    return jnp.dot(a, b, preferred_element_type=F32)


def _params(*sem):
    return pltpu.CompilerParams(dimension_semantics=sem, vmem_limit_bytes=VMEM_LIMIT_BYTES)


def _resident(shape):
    return pl.BlockSpec(shape, lambda *_: (0,) * len(shape), pipeline_mode=pl.Buffered(1))


def _ffn_body(final_norm, x_ref, g_ref, wg_ref, wu_ref, wd_ref, *rest):
    if final_norm:
        gf_ref, o_ref, h_ref = rest
    else:
        o_ref, h_ref = rest
    j = pl.program_id(1)

    @pl.when(j == 0)
    def _():
        x = x_ref[...]
        h_ref[...] = _rms(x, g_ref[...]).astype(BF16)
        o_ref[...] = x

    h = h_ref[...]
    acts = []
    for c in range(0, FFN_TF, FFN_TF // 2):
        cols = slice(c, c + FFN_TF // 2)
        gate = _dot(h, wg_ref[:, cols])
        up = _dot(h, wu_ref[:, cols])
        acts.append((0.5 * gate * jax.nn.sigmoid(gate) * up).astype(BF16))
    o_ref[...] += _dot(jnp.concatenate(acts, axis=1), wd_ref[...])

    if final_norm:
        @pl.when(j == pl.num_programs(1) - 1)
        def _():
            o_ref[...] = _rms(o_ref[...], gf_ref[...])


def _ffn(x, g, wg, wu, wd, g_final=None):
    t = x.shape[0]
    final_norm = g_final is not None
    row = pl.BlockSpec((FFN_TM, D_MODEL), lambda i, j: (i, 0))
    gain = pl.BlockSpec((1, D_MODEL), lambda i, j: (0, 0))
    in_specs = [row, gain,
                pl.BlockSpec((D_MODEL, FFN_TF), lambda i, j: (0, j)),
                pl.BlockSpec((D_MODEL, FFN_TF), lambda i, j: (0, j)),
                pl.BlockSpec((FFN_TF, D_MODEL), lambda i, j: (j, 0))]
    args = [x, g, wg, wu, wd]
    if final_norm:
        in_specs.append(gain)
        args.append(g_final)
    return pl.pallas_call(
        functools.partial(_ffn_body, final_norm),
        grid=(t // FFN_TM, D_FF // FFN_TF),
        in_specs=in_specs,
        out_specs=row,
        out_shape=jax.ShapeDtypeStruct((t, D_MODEL), F32),
        scratch_shapes=[pltpu.VMEM((FFN_TM, D_MODEL), BF16)],
        compiler_params=_params("parallel", "arbitrary"),
        name="ffn_final" if final_norm else "ffn",
    )(*args)


def _mix_body(x_ref, gmix_ref, win_ref, gq_ref, wq_ref, gkv_ref, wkv_ref, tab_ref,
              gv_ref, ws_ref, bs_ref, gog_ref,
              q_ref, k_ref, vt_ref, og_ref, og_scr):
    scale = math.log2(math.e) / math.sqrt(NOPE + ROPE)
    hm =_rms(x_ref[...], gmix_ref[...]).astype(BF16)
    z = _dot(hm, win_ref[...])
    cq = _rms(z[:, C_Q0:C_Q0 + Q_LORA], gq_ref[...]).astype(BF16)
    ckv = _rms(z[:, C_KV0:C_KV0 + KV_LORA], gkv_ref[...]).astype(BF16)
    q = _dot(cq, wq_ref[...])
    kv = _dot(ckv, wkv_ref[...])
    cos_t = tab_ref[:, :LANES]
    sin_t = tab_ref[:, LANES:]
    kr_f = (z[:, C_KR0:C_KR0 + LANES] * cos_t
            + z[:, C_KR0 + LANES:C_KR0 + 2 * LANES] * sin_t)
    kr = (kr_f.astype(BF16), pltpu.roll(kr_f, ROPE, axis=1).astype(BF16))
    for pair in range(PAIRS):
        c = pair * LANES
        qr = (q[:, Q_ROPE0 + c:Q_ROPE0 + c + LANES] * cos_t
              + q[:, Q_PART0 + c:Q_PART0 + c + LANES] * sin_t)
        qr = (qr * scale).astype(BF16)
        for h in (2 * pair, 2 * pair + 1):
            c = h * QK_PAD
            q_ref[:, c:c + LANES] = (q[:, h * NOPE:(h + 1) * NOPE] * scale).astype(BF16)
            q_ref[:, c + LANES:c + QK_PAD] = qr
            k_ref[:, c:c + LANES] = kv[:, c:c + LANES].astype(BF16)
            k_ref[:, c + LANES:c + QK_PAD] = kr[h % 2]
            vt_ref[h] = kv[:, c + LANES:c + QK_PAD].T.astype(BF16)

    tm = x_ref.shape[0]
    for g in range(GROUPS):
        cg = g * GROUP_DIM
        vg = jax.nn.gelu(z[:, C_V0 + cg:C_V0 + cg + GROUP_DIM])
        vn = _rms(vg, gv_ref[:, cg:cg + GROUP_DIM]).astype(BF16)
        ug = jax.nn.gelu(z[:, C_U0 + cg:C_U0 + cg + GROUP_DIM])
        bias = bs_ref[:, g:g + 1]
        w = ws_ref[g]
        for n in range(0, tm // CHUNK, 2):
            r0 = slice(n * CHUNK, (n + 1) * CHUNK)
            r1 = slice((n + 1) * CHUNK, (n + 2) * CHUNK)
            s = _dot(w, jnp.concatenate([vn[r0], vn[r1]], axis=1))
            og_scr[r0, cg:cg + GROUP_DIM] = ug[r0] * (s[:, :GROUP_DIM] + bias)
            og_scr[r1, cg:cg + GROUP_DIM] = ug[r1] * (s[:, GROUP_DIM:] + bias)
    og_ref[...] = _rms(og_scr[...], gog_ref[...]).astype(BF16)


def _mix(x1, seq, g_mix, w_in, g_q, w_q, g_kv, w_kv, tab, g_v, w_s, b_t, g_og):
    t = x1.shape[0]
    tm = MIX_TM
    blocks_per_seq = seq // tm
    row = lambda w: pl.BlockSpec((tm, w), lambda i: (i, 0))
    return pl.pallas_call(
        _mix_body,
        grid=(t // tm,),
        in_specs=[row(D_MODEL), _resident((1, D_MODEL)), _resident((D_MODEL, IN_COLS)),
                  _resident((1, Q_LORA)), _resident((Q_LORA, Q_COLS)),
                  _resident((1, KV_LORA)), _resident((KV_LORA, HEADS * QK_PAD)),
                  pl.BlockSpec((tm, 2 * LANES), lambda i: (i % blocks_per_seq, 0)),
                  _resident((1, GMLP_WIDTH)), _resident((GROUPS, CHUNK, CHUNK)),
                  _resident((CHUNK, GROUPS)), _resident((1, GMLP_WIDTH))],
        out_specs=[row(HEADS * QK_PAD), row(HEADS * QK_PAD),
                   pl.BlockSpec((HEADS, V_DIM, tm), lambda i: (0, 0, i)), row(GMLP_WIDTH)],
        out_shape=[jax.ShapeDtypeStruct((t, HEADS * QK_PAD), BF16),
                   jax.ShapeDtypeStruct((t, HEADS * QK_PAD), BF16),
                   jax.ShapeDtypeStruct((HEADS, V_DIM, t), BF16),
                   jax.ShapeDtypeStruct((t, GMLP_WIDTH), BF16)],
        scratch_shapes=[pltpu.VMEM((tm, GMLP_WIDTH), F32)],
        compiler_params=_params("parallel"),
        name="mix",
    )(x1, g_mix, w_in, g_q, w_q, g_kv, w_kv, tab, g_v, w_s, b_t, g_og)


def _attn_body(nsub, tk, q_ref, k_ref, vt_ref, o_ref, s_scr):
    tq = q_ref.shape[0] // nsub
    seq = k_ref.shape[0]
    nchunk = seq // tk
    ahead = min(ATTN_AHEAD, nchunk)

    def fold(x, op):
        return functools.reduce(op, [x[r:r + SUBLANES] for r in range(0, tk, SUBLANES)])

    def score_chunk(i, c):
        rows = slice(c * tk, (c + 1) * tk)
        s = lax.dot_general(k_ref[rows, :], q_ref[i * tq:(i + 1) * tq, :],
                            (((1,), (1,)), ((), ())), preferred_element_type=F32)
        s_scr[i % 2, rows, :] = s
        return fold(s, jnp.maximum)

    maxes = [score_chunk(0, c) for c in range(nchunk)]
    for i in range(nsub):
        m = jnp.max(functools.reduce(jnp.maximum, maxes), axis=0, keepdims=True)
        more = i + 1 < nsub
        maxes = [score_chunk(i + 1, c) for c in range(ahead)] if more else []
        lsum = acc = None
        for c in range(nchunk):
            rows = slice(c * tk, (c + 1) * tk)
            if more and c + ahead < nchunk:
                maxes.append(score_chunk(i + 1, c + ahead))
            p = jnp.exp2(s_scr[i % 2, rows, :] - m)
            ps = fold(p, jnp.add)
            pv = _dot(vt_ref[:, rows], p.astype(BF16))
            lsum = ps if lsum is None else lsum + ps
            acc = pv if acc is None else acc + pv
        o_t = acc / jnp.sum(lsum, axis=0, keepdims=True)
        o_ref[i * tq:(i + 1) * tq, :] = o_t.T


def _attn(q, k, v, batch, seq):
    tq, nsub, tk = ATTN_BLOCKING[seq]
    rows = tq * nsub
    nq = seq // rows
    return pl.pallas_call(
        functools.partial(_attn_body, nsub, tk),
        grid=(batch, HEADS, nq),
        in_specs=[pl.BlockSpec((rows, QK_PAD), lambda b, h, i: (b * nq + i, h)),
                  pl.BlockSpec((seq, QK_PAD), lambda b, h, i: (b, h)),
                  pl.BlockSpec((None, V_DIM, seq), lambda b, h, i: (h, 0, b))],
        out_specs=pl.BlockSpec((rows, V_DIM), lambda b, h, i: (b * nq + i, h)),
        out_shape=jax.ShapeDtypeStruct((batch * seq, MLA_WIDTH), F32),
        scratch_shapes=[pltpu.VMEM((2, seq, tq), F32)],
        compiler_params=_params("parallel", "parallel", "arbitrary"),
        name="attn",
    )(q, k, v)


def _merge_body(x_ref, oa_ref, og_ref, goa_ref, wo_ref, o_ref):
    on = _rms(oa_ref[...], goa_ref[...]).astype(BF16)
    o_ref[...] = (x_ref[...] + _dot(on, wo_ref[:MLA_WIDTH, :])
                  + _dot(og_ref[...], wo_ref[MLA_WIDTH:, :]))


def _merge(x1, oa, og, g_oa, w_out):
    t = x1.shape[0]
    tm = MERGE_TM
    row = lambda w: pl.BlockSpec((tm, w), lambda i: (i, 0))
    return pl.pallas_call(
        _merge_body,
        grid=(t // tm,),
        in_specs=[row(D_MODEL), row(MLA_WIDTH), row(GMLP_WIDTH),
                  _resident((1, MLA_WIDTH)), _resident((MLA_WIDTH + GMLP_WIDTH, D_MODEL))],
        out_specs=row(D_MODEL),
        out_shape=jax.ShapeDtypeStruct((t, D_MODEL), F32),
        compiler_params=_params("parallel"),
        name="merge",
    )(x1, oa, og, g_oa, w_out)


def _rope_table(seq):
    inv = 1.0 / (ROPE_THETA ** (jnp.arange(0, ROPE, 2, dtype=F32) / ROPE))
    ang = jnp.arange(seq, dtype=F32)[:, None] * inv[None, :]
    reps = LANES // HALF_ROPE
    return jnp.concatenate([jnp.tile(jnp.cos(ang), (1, reps)),
                            jnp.tile(jnp.sin(ang), (1, reps))], axis=1)


def _relay_w_in(w_in):
    r1 = w_in[:, C_KR0:C_KR0 + HALF_ROPE]
    r2 = w_in[:, C_KR0 + HALF_ROPE:C_KR0 + ROPE]
    pad = jnp.zeros((D_MODEL, LANES - ROPE), w_in.dtype)
    return jnp.concatenate(
        [w_in[:, :C_KR0], r1, r2, pad, -r2, r1, pad, w_in[:, C_KR0 + ROPE:]], axis=1).astype(BF16)


def _relay_w_q(w_q):
    w = w_q.reshape(Q_LORA, HEADS, NOPE + ROPE)
    nope, r1, r2 = w[..., :NOPE], w[..., NOPE:NOPE + HALF_ROPE], w[..., NOPE + HALF_ROPE:]
    nope = nope.reshape(Q_LORA, HEADS * NOPE)
    rope = jnp.concatenate([r1, r2], axis=-1).reshape(Q_LORA, PAIRS * LANES)
    partner = jnp.concatenate([-r2, r1], axis=-1).reshape(Q_LORA, PAIRS * LANES)
    return jnp.concatenate([nope, rope, partner], axis=1).astype(BF16)


def _layer(x, seq, w):
    batch = x.shape[0]
    x = x.reshape(batch * seq, D_MODEL)
    x1 = _ffn(x, w["g_ffn1"], w["w1_gate"], w["w1_up"], w["w1_down"])
    q, k, v, og = _mix(x1, seq, w["g_mix"], w["w_in"], w["g_q"], w["w_q"], w["g_kv"], w["w_kv"],
                       _rope_table(seq), w["g_v"], w["w_s"], w["b_t"], w["g_og"])
    oa = _attn(q, k, v, batch, seq)
    x2 = _merge(x1, oa, og, w["g_oa"], w["w_out"])
    y = _ffn(x2, w["g_ffn2"], w["w2_gate"], w["w2_up"], w["w2_down"], w["g_final"])
    return y.reshape(batch, seq, D_MODEL)


def kernel(x_prompt, x_sample, g_ffn1, w1_gate, w1_up, w1_down, g_mix, w_in, g_q, w_q_b, g_kv,
           w_kv_b, g_v, w_s, b_s, g_out_attn, g_out_gmlp, w_out, g_ffn2, w2_gate, w2_up,
           w2_down, g_final):
    row = lambda g: g.reshape(1, -1)
    w = dict(
        g_ffn1=row(g_ffn1[0]), w1_gate=w1_gate[0].astype(BF16), w1_up=w1_up[0].astype(BF16),
        w1_down=w1_down[0].astype(BF16),
        g_mix=row(g_mix[0]), w_in=_relay_w_in(w_in[0]),
        g_q=row(g_q[0]), w_q=_relay_w_q(w_q_b[0]),
        g_kv=row(g_kv[0]), w_kv=w_kv_b[0].astype(BF16),
        g_v=row(g_v[0]), w_s=w_s[0].astype(BF16), b_t=jnp.transpose(b_s[0]),
        g_og=row(g_out_gmlp[0]), g_oa=row(g_out_attn[0]), w_out=w_out[0].astype(BF16),
        g_ffn2=row(g_ffn2[0]), w2_gate=w2_gate[0].astype(BF16), w2_up=w2_up[0].astype(BF16),
        w2_down=w2_down[0].astype(BF16), g_final=row(g_final),
    )
    y_prompt = _layer(x_prompt, x_prompt.shape[1], w)
    y_sample = _layer(x_sample, x_sample.shape[1], w)
    return (y_prompt, y_sample)
```

```python
import functools
import math

import jax
import jax.numpy as jnp
from jax import lax
from jax.experimental import pallas as pl
from jax.experimental.pallas import tpu as pltpu

F32 = jnp.float32
BF16 = jnp.bfloat16

D_MODEL = 2048
HEADS = 8
NOPE = 128
ROPE = 64
HALF_ROPE = ROPE // 2
V_DIM = 128
Q_LORA = 512
KV_LORA = 512
GROUPS = 8
GROUP_DIM = 128
GMLP_WIDTH = GROUPS * GROUP_DIM
MLA_WIDTH = HEADS * V_DIM
CHUNK = 128
D_FF = 5632
ROPE_THETA = 10000.0
EPS = 1e-6

LANES = 128
SUBLANES = 8
ATTN_AHEAD = 2
QK_PAD = 2 * LANES
PAIRS = HEADS // 2
C_Q0, C_KV0 = 0, Q_LORA
C_KR0 = Q_LORA + KV_LORA
C_U0 = C_KR0 + 2 * LANES
C_V0 = C_U0 + GMLP_WIDTH
IN_COLS = C_V0 + GMLP_WIDTH
Q_NOPE0, Q_ROPE0, Q_PART0 = 0, HEADS * NOPE, HEADS * NOPE + PAIRS * LANES
Q_COLS = Q_PART0 + PAIRS * LANES

VMEM_LIMIT_BYTES = 60000 * 1024

FFN_TM, FFN_TF = 1024, 512
MIX_TM, MIX_SUB = 512, 256
MERGE_TM = 512
ATTN_BLOCKING = {8192: (256, 8, 1024, 1), 2048: (512, 4, 1024, 2)}


def _rms(x, g):
    return x * lax.rsqrt(jnp.mean(x * x, axis=-1, keepdims=True) + EPS) * g


def _dot(a, b):
    return jnp.dot(a, b, preferred_element_type=F32)


def _params(*sem):
    return pltpu.CompilerParams(dimension_semantics=sem, vmem_limit_bytes=VMEM_LIMIT_BYTES)


def _resident(shape):
    return pl.BlockSpec(shape, lambda *_: (0,) * len(shape), pipeline_mode=pl.Buffered(1))


def _ffn_body(final_norm, x_ref, g_ref, wg_ref, wu_ref, wd_ref, *rest):
    if final_norm:
        gf_ref, o_ref, h_ref = rest
    else:
        o_ref, h_ref = rest
    j = pl.program_id(1)

    @pl.when(j == 0)
    def _():
        x = x_ref[...]
        h_ref[...] = _rms(x, g_ref[...]).astype(BF16)
        o_ref[...] = x

    h = h_ref[...]
    acts = []
    for c in range(0, FFN_TF, FFN_TF // 2):
        cols = slice(c, c + FFN_TF // 2)
        gate = _dot(h, wg_ref[:, cols])
        up = _dot(h, wu_ref[:, cols])
        acts.append((0.5 * gate * jax.nn.sigmoid(gate) * up).astype(BF16))
    o_ref[...] += _dot(jnp.concatenate(acts, axis=1), wd_ref[...])

    if final_norm:
        @pl.when(j == pl.num_programs(1) - 1)
        def _():
            o_ref[...] = _rms(o_ref[...], gf_ref[...])


def _ffn(x, g, wg, wu, wd, g_final=None):
    t = x.shape[0]
    final_norm = g_final is not None
    row = pl.BlockSpec((FFN_TM, D_MODEL), lambda i, j: (i, 0))
    gain = pl.BlockSpec((1, D_MODEL), lambda i, j: (0, 0))
    in_specs = [row, gain,
                pl.BlockSpec((D_MODEL, FFN_TF), lambda i, j: (0, j)),
                pl.BlockSpec((D_MODEL, FFN_TF), lambda i, j: (0, j)),
                pl.BlockSpec((FFN_TF, D_MODEL), lambda i, j: (j, 0))]
    args = [x, g, wg, wu, wd]
    if final_norm:
        in_specs.append(gain)
        args.append(g_final)
    return pl.pallas_call(
        functools.partial(_ffn_body, final_norm),
        grid=(t // FFN_TM, D_FF // FFN_TF),
        in_specs=in_specs,
        out_specs=row,
        out_shape=jax.ShapeDtypeStruct((t, D_MODEL), F32),
        scratch_shapes=[pltpu.VMEM((FFN_TM, D_MODEL), BF16)],
        compiler_params=_params("parallel", "arbitrary"),
        name="ffn_final" if final_norm else "ffn",
    )(*args)


def _mix_body(x_ref, gmix_ref, win_ref, gq_ref, wq_ref, gkv_ref, wkv_ref, tab_ref,
              gv_ref, ws_ref, bs_ref, gog_ref,
              q_ref, k_ref, vt_ref, og_ref, og_scr):
    for t0 in range(0, x_ref.shape[0], MIX_SUB):
        _mix_rows(slice(t0, t0 + MIX_SUB), x_ref, gmix_ref, win_ref, gq_ref, wq_ref, gkv_ref,
                  wkv_ref, tab_ref, gv_ref, ws_ref, bs_ref, gog_ref,
                  q_ref, k_ref, vt_ref, og_ref, og_scr)


def _mix_rows(rws, x_ref, gmix_ref, win_ref, gq_ref, wq_ref, gkv_ref, wkv_ref, tab_ref,
              gv_ref, ws_ref, bs_ref, gog_ref, q_ref, k_ref, vt_ref, og_ref, og_scr):
    scale = math.log2(math.e) / math.sqrt(NOPE + ROPE)
    hm = _rms(x_ref[rws, :], gmix_ref[...]).astype(BF16)
    z = _dot(hm, win_ref[...])
    cq = _rms(z[:, C_Q0:C_Q0 + Q_LORA], gq_ref[...]).astype(BF16)
    ckv = _rms(z[:, C_KV0:C_KV0 + KV_LORA], gkv_ref[...]).astype(BF16)
    q = _dot(cq, wq_ref[...])
    kv = _dot(ckv, wkv_ref[...])
    cos_t = tab_ref[rws, :LANES]
    sin_t = tab_ref[rws, LANES:]
    kr_f = (z[:, C_KR0:C_KR0 + LANES] * cos_t
            + z[:, C_KR0 + LANES:C_KR0 + 2 * LANES] * sin_t)
    kr = (kr_f.astype(BF16), pltpu.roll(kr_f, ROPE, axis=1).astype(BF16))
    for pair in range(PAIRS):
        c = pair * LANES
        qr = (q[:, Q_ROPE0 + c:Q_ROPE0 + c + LANES] * cos_t
              + q[:, Q_PART0 + c:Q_PART0 + c + LANES] * sin_t)
        qr = (qr * scale).astype(BF16)
        for h in (2 * pair, 2 * pair + 1):
            c = h * QK_PAD
            q_ref[rws, c:c + LANES] = (q[:, h * NOPE:(h + 1) * NOPE] * scale).astype(BF16)
            q_ref[rws, c + LANES:c + QK_PAD] = qr
            k_ref[rws, c:c + LANES] = kv[:, c:c + LANES].astype(BF16)
            k_ref[rws, c + LANES:c + QK_PAD] = kr[h % 2]
            vt_ref[h, :, rws] = kv[:, c + LANES:c + QK_PAD].T.astype(BF16)

    for g in range(GROUPS):
        cg = g * GROUP_DIM
        vg = jax.nn.gelu(z[:, C_V0 + cg:C_V0 + cg + GROUP_DIM])
        vn = _rms(vg, gv_ref[:, cg:cg + GROUP_DIM]).astype(BF16)
        ug = jax.nn.gelu(z[:, C_U0 + cg:C_U0 + cg + GROUP_DIM])
        bias = bs_ref[:, g:g + 1]
        w = ws_ref[g]
        for n in range(0, MIX_SUB, 2 * CHUNK):
            r0, r1 = slice(n, n + CHUNK), slice(n + CHUNK, n + 2 * CHUNK)
            s = _dot(w, jnp.concatenate([vn[r0], vn[r1]], axis=1))
            t0 = rws.start + n
            og_scr[t0:t0 + CHUNK, cg:cg + GROUP_DIM] = ug[r0] * (s[:, :GROUP_DIM] + bias)
            og_scr[t0 + CHUNK:t0 + 2 * CHUNK, cg:cg + GROUP_DIM] = ug[r1] * (s[:, GROUP_DIM:] + bias)
    og_ref[rws, :] = _rms(og_scr[rws, :], gog_ref[...]).astype(BF16)


def _mix(x1, seq, g_mix, w_in, g_q, w_q, g_kv, w_kv, tab, g_v, w_s, b_t, g_og):
    t = x1.shape[0]
    tm = MIX_TM
    blocks_per_seq = seq // tm
    row = lambda w: pl.BlockSpec((tm, w), lambda i: (i, 0))
    return pl.pallas_call(
        _mix_body,
        grid=(t // tm,),
        in_specs=[row(D_MODEL), _resident((1, D_MODEL)), _resident((D_MODEL, IN_COLS)),
                  _resident((1, Q_LORA)), _resident((Q_LORA, Q_COLS)),
                  _resident((1, KV_LORA)), _resident((KV_LORA, HEADS * QK_PAD)),
                  pl.BlockSpec((tm, 2 * LANES), lambda i: (i % blocks_per_seq, 0)),
                  _resident((1, GMLP_WIDTH)), _resident((GROUPS, CHUNK, CHUNK)),
                  _resident((CHUNK, GROUPS)), _resident((1, GMLP_WIDTH))],
        out_specs=[row(HEADS * QK_PAD), row(HEADS * QK_PAD),
                   pl.BlockSpec((HEADS, V_DIM, tm), lambda i: (0, 0, i)), row(GMLP_WIDTH)],
        out_shape=[jax.ShapeDtypeStruct((t, HEADS * QK_PAD), BF16),
                   jax.ShapeDtypeStruct((t, HEADS * QK_PAD), BF16),
                   jax.ShapeDtypeStruct((HEADS, V_DIM, t), BF16),
                   jax.ShapeDtypeStruct((t, GMLP_WIDTH), BF16)],
        scratch_shapes=[pltpu.VMEM((tm, GMLP_WIDTH), F32)],
        compiler_params=_params("parallel"),
        name="mix",
    )(x1, g_mix, w_in, g_q, w_q, g_kv, w_kv, tab, g_v, w_s, b_t, g_og)


def _attn_body(nsub, tk, q_ref, k_ref, vt_ref, o_ref, s_scr):
    heads = vt_ref.shape[0]
    tq = q_ref.shape[0] // nsub
    seq = k_ref.shape[0]
    nchunk = seq // tk
    ahead = min(ATTN_AHEAD, nchunk)
    units = [(h, i) for h in range(heads) for i in range(nsub)]

    def fold(x, op):
        return functools.reduce(op, [x[r:r + SUBLANES] for r in range(0, tk, SUBLANES)])

    def score_chunk(u, c):
        h, i = units[u]
        rows = slice(c * tk, (c + 1) * tk)
        s = lax.dot_general(k_ref[rows, h * QK_PAD:(h + 1) * QK_PAD],
                            q_ref[i * tq:(i + 1) * tq, h * QK_PAD:(h + 1) * QK_PAD],
                            (((1,), (1,)), ((), ())), preferred_element_type=F32)
        s_scr[u % 2, rows, :] = s
        return fold(s, jnp.maximum)

    maxes = [score_chunk(0, c) for c in range(nchunk)]
    for u, (h, i) in enumerate(units):
        m = jnp.max(functools.reduce(jnp.maximum, maxes), axis=0, keepdims=True)
        more = u + 1 < len(units)
        maxes = [score_chunk(u + 1, c) for c in range(ahead)] if more else []
        lsum = acc = None
        for c in range(nchunk):
            rows = slice(c * tk, (c + 1) * tk)
            if more and c + ahead < nchunk:
                maxes.append(score_chunk(u + 1, c + ahead))
            p = jnp.exp2(s_scr[u % 2, rows, :] - m)
            ps = fold(p, jnp.add)
            pv = _dot(vt_ref[h, :, rows], p.astype(BF16))
            lsum = ps if lsum is None else lsum + ps
            acc = pv if acc is None else acc + pv
        o_t = acc / jnp.sum(lsum, axis=0, keepdims=True)
        o_ref[i * tq:(i + 1) * tq, h * V_DIM:(h + 1) * V_DIM] = o_t.T


def _attn(q, k, v, batch, seq):
    tq, nsub, tk, hp = ATTN_BLOCKING[seq]
    rows = tq * nsub
    nq = seq // rows
    return pl.pallas_call(
        functools.partial(_attn_body, nsub, tk),
        grid=(batch, HEADS // hp, nq),
        in_specs=[pl.BlockSpec((rows, hp * QK_PAD), lambda b, h, i: (b * nq + i, h)),
                  pl.BlockSpec((seq, hp * QK_PAD), lambda b, h, i: (b, h)),
                  pl.BlockSpec((hp, V_DIM, seq), lambda b, h, i: (h, 0, b))],
        out_specs=pl.BlockSpec((rows, hp * V_DIM), lambda b, h, i: (b * nq + i, h)),
        out_shape=jax.ShapeDtypeStruct((batch * seq, MLA_WIDTH), F32),
        scratch_shapes=[pltpu.VMEM((2, seq, tq), F32)],
        compiler_params=_params("parallel", "parallel", "arbitrary"),
        name="attn",
    )(q, k, v)


def _merge_body(x_ref, oa_ref, og_ref, goa_ref, wo_ref, o_ref):
    on = _rms(oa_ref[...], goa_ref[...]).astype(BF16)
    o_ref[...] = (x_ref[...] + _dot(on, wo_ref[:MLA_WIDTH, :])
                  + _dot(og_ref[...], wo_ref[MLA_WIDTH:, :]))


def _merge(x1, oa, og, g_oa, w_out):
    t = x1.shape[0]
    tm = MERGE_TM
    row = lambda w: pl.BlockSpec((tm, w), lambda i: (i, 0))
    return pl.pallas_call(
        _merge_body,
        grid=(t // tm,),
        in_specs=[row(D_MODEL), row(MLA_WIDTH), row(GMLP_WIDTH),
                  _resident((1, MLA_WIDTH)), _resident((MLA_WIDTH + GMLP_WIDTH, D_MODEL))],
        out_specs=row(D_MODEL),
        out_shape=jax.ShapeDtypeStruct((t, D_MODEL), F32),
        compiler_params=_params("parallel"),
        name="merge",
    )(x1, oa, og, g_oa, w_out)


def _rope_table(seq):
    inv = 1.0 / (ROPE_THETA ** (jnp.arange(0, ROPE, 2, dtype=F32) / ROPE))
    ang = jnp.arange(seq, dtype=F32)[:, None] * inv[None, :]
    reps = LANES // HALF_ROPE
    return jnp.concatenate([jnp.tile(jnp.cos(ang), (1, reps)),
                            jnp.tile(jnp.sin(ang), (1, reps))], axis=1)


def _relay_w_in(w_in):
    r1 = w_in[:, C_KR0:C_KR0 + HALF_ROPE]
    r2 = w_in[:, C_KR0 + HALF_ROPE:C_KR0 + ROPE]
    pad = jnp.zeros((D_MODEL, LANES - ROPE), w_in.dtype)
    return jnp.concatenate(
        [w_in[:, :C_KR0], r1, r2, pad, -r2, r1, pad, w_in[:, C_KR0 + ROPE:]], axis=1).astype(BF16)


def _relay_w_q(w_q):
    w = w_q.reshape(Q_LORA, HEADS, NOPE + ROPE)
    nope, r1, r2 = w[..., :NOPE], w[..., NOPE:NOPE + HALF_ROPE], w[..., NOPE + HALF_ROPE:]
    nope = nope.reshape(Q_LORA, HEADS * NOPE)
    rope = jnp.concatenate([r1, r2], axis=-1).reshape(Q_LORA, PAIRS * LANES)
    partner = jnp.concatenate([-r2, r1], axis=-1).reshape(Q_LORA, PAIRS * LANES)
    return jnp.concatenate([nope, rope, partner], axis=1).astype(BF16)


def _layer(x, seq, w):
    batch = x.shape[0]
    x = x.reshape(batch * seq, D_MODEL)
    x1 = _ffn(x, w["g_ffn1"], w["w1_gate"], w["w1_up"], w["w1_down"])
    q, k, v, og = _mix(x1, seq, w["g_mix"], w["w_in"], w["g_q"], w["w_q"], w["g_kv"], w["w_kv"],
                       _rope_table(seq), w["g_v"], w["w_s"], w["b_t"], w["g_og"])
    oa = _attn(q, k, v, batch, seq)
    x2 = _merge(x1, oa, og, w["g_oa"], w["w_out"])
    y = _ffn(x2, w["g_ffn2"], w["w2_gate"], w["w2_up"], w["w2_down"], w["g_final"])
    return y.reshape(batch, seq, D_MODEL)


def kernel(x_prompt, x_sample, g_ffn1, w1_gate, w1_up, w1_down, g_mix, w_in, g_q, w_q_b, g_kv,
           w_kv_b, g_v, w_s, b_s, g_out_attn, g_out_gmlp, w_out, g_ffn2, w2_gate, w2_up,
           w2_down, g_final):
    row = lambda g: g.reshape(1, -1)
    w = dict(
        g_ffn1=row(g_ffn1[0]), w1_gate=w1_gate[0].astype(BF16), w1_up=w1_up[0].astype(BF16),
        w1_down=w1_down[0].astype(BF16),
        g_mix=row(g_mix[0]), w_in=_relay_w_in(w_in[0]),
        g_q=row(g_q[0]), w_q=_relay_w_q(w_q_b[0]),
        g_kv=row(g_kv[0]), w_kv=w_kv_b[0].astype(BF16),
        g_v=row(g_v[0]), w_s=w_s[0].astype(BF16), b_t=jnp.transpose(b_s[0]),
        g_og=row(g_out_gmlp[0]), g_oa=row(g_out_attn[0]), w_out=w_out[0].astype(BF16),
        g_ffn2=row(g_ffn2[0]), w2_gate=w2_gate[0].astype(BF16), w2_up=w2_up[0].astype(BF16),
        w2_down=w2_down[0].astype(BF16), g_final=row(g_final),
    )
    y_prompt = _layer(x_prompt, x_prompt.shape[1], w)
    y_sample = _layer(x_sample, x_sample.shape[1], w)
    return (y_prompt, y_sample)
```

```python
import functools
import math

import jax
import jax.numpy as jnp
from jax import lax
from jax.experimental import pallas as pl
from jax.experimental.pallas import tpu as pltpu

F32 = jnp.float32
BF16 = jnp.bfloat16

D_MODEL = 2048
HEADS = 8
NOPE = 128
ROPE = 64
HALF_ROPE = ROPE // 2
V_DIM = 128
Q_LORA = 512
KV_LORA = 512
GROUPS = 8
GROUP_DIM = 128
GMLP_WIDTH = GROUPS * GROUP_DIM
MLA_WIDTH = HEADS * V_DIM
CHUNK = 128
D_FF = 5632
ROPE_THETA = 10000.0
EPS = 1e-6

LANES = 128
SUBLANES = 8
ATTN_AHEAD = 2
QK_PAD = 2 * LANES
PAIRS = HEADS // 2
C_Q0, C_KV0 = 0, Q_LORA
C_KR0 = Q_LORA + KV_LORA
C_U0 = C_KR0 + 2 * LANES
C_V0 = C_U0 + GMLP_WIDTH
IN_COLS = C_V0 + GMLP_WIDTH
Q_NOPE0, Q_ROPE0, Q_PART0 = 0, HEADS * NOPE, HEADS * NOPE + PAIRS * LANES
Q_COLS = Q_PART0 + PAIRS * LANES

VMEM_LIMIT_BYTES = 60000 * 1024

FFN_TM, FFN_TF = 1024, 512
MIX_TM, MIX_SUB = 512, 256
MERGE_TM = 512
ATTN_BLOCKING = {8192: (256, 8, 1024, 1), 2048: (512, 4, 1024, 2)}


def _rms(x, g):
    return x * lax.rsqrt(jnp.mean(x * x, axis=-1, keepdims=True) + EPS) * g


def _dot(a, b):
    return jnp.dot(a, b, preferred_element_type=F32)


def _params(*sem):
    return pltpu.CompilerParams(dimension_semantics=sem, vmem_limit_bytes=VMEM_LIMIT_BYTES)


def _resident(shape):
    return pl.BlockSpec(shape, lambda *_: (0,) * len(shape), pipeline_mode=pl.Buffered(1))


def _ffn_body(final_norm, x_ref, g_ref, wg_ref, wu_ref, wd_ref, *rest):
    if final_norm:
        gf_ref, o_ref, h_ref = rest
    else:
        o_ref, h_ref = rest
    j = pl.program_id(1)

    @pl.when(j == 0)
    def _():
        x = x_ref[...]
        h_ref[...] = _rms(x, g_ref[...]).astype(BF16)
        o_ref[...] = x

    h = h_ref[...]
    acts = []
    for c in range(0, FFN_TF, FFN_TF // 2):
        cols = slice(c, c + FFN_TF // 2)
        gate = _dot(h, wg_ref[:, cols])
        up = _dot(h, wu_ref[:, cols])
        acts.append((0.5 * gate * jax.nn.sigmoid(gate) * up).astype(BF16))
    o_ref[...] += _dot(jnp.concatenate(acts, axis=1), wd_ref[...])

    if final_norm:
        @pl.when(j == pl.num_programs(1) - 1)
        def _():
            o_ref[...] = _rms(o_ref[...], gf_ref[...])


def _ffn(x, g, wg, wu, wd, g_final=None):
    t = x.shape[0]
    final_norm = g_final is not None
    row = pl.BlockSpec((FFN_TM, D_MODEL), lambda i, j: (i, 0))
    gain = pl.BlockSpec((1, D_MODEL), lambda i, j: (0, 0))
    in_specs = [row, gain,
                pl.BlockSpec((D_MODEL, FFN_TF), lambda i, j: (0, j)),
                pl.BlockSpec((D_MODEL, FFN_TF), lambda i, j: (0, j)),
                pl.BlockSpec((FFN_TF, D_MODEL), lambda i, j: (j, 0))]
    args = [x, g, wg, wu, wd]
    if final_norm:
        in_specs.append(gain)
        args.append(g_final)
    return pl.pallas_call(
        functools.partial(_ffn_body, final_norm),
        grid=(t // FFN_TM, D_FF // FFN_TF),
        in_specs=in_specs,
        out_specs=row,
        out_shape=jax.ShapeDtypeStruct((t, D_MODEL), F32),
        scratch_shapes=[pltpu.VMEM((FFN_TM, D_MODEL), BF16)],
        compiler_params=_params("parallel", "arbitrary"),
        name="ffn_final" if final_norm else "ffn",
    )(*args)


def _mix_body(x_ref, gmix_ref, win_ref, gq_ref, wq_ref, gkv_ref, wkv_ref, tab_ref,
              gv_ref, ws_ref, bs_ref, gog_ref,
              q_ref, k_ref, vt_ref, og_ref, og_scr):
    for t0 in range(0, x_ref.shape[0], MIX_SUB):
        _mix_rows(slice(t0, t0 + MIX_SUB), x_ref, gmix_ref, win_ref, gq_ref, wq_ref, gkv_ref,
                  wkv_ref, tab_ref, gv_ref, ws_ref, bs_ref, gog_ref,
                  q_ref, k_ref, vt_ref, og_ref, og_scr)


def _mix_rows(rws, x_ref, gmix_ref, win_ref, gq_ref, wq_ref, gkv_ref, wkv_ref, tab_ref,
              gv_ref, ws_ref, bs_ref, gog_ref, q_ref, k_ref, vt_ref, og_ref, og_scr):
    scale = math.log2(math.e) / math.sqrt(NOPE + ROPE)
    hm = _rms(x_ref[rws, :], gmix_ref[...]).astype(BF16)
    z = _dot(hm, win_ref[...])
    cq = _rms(z[:, C_Q0:C_Q0 + Q_LORA], gq_ref[...]).astype(BF16)
    ckv = _rms(z[:, C_KV0:C_KV0 + KV_LORA], gkv_ref[...]).astype(BF16)
    q = _dot(cq, wq_ref[...])
    kv = _dot(ckv, wkv_ref[...])
    cos_t = tab_ref[rws, :LANES]
    sin_t = tab_ref[rws, LANES:]
    kr_f = (z[:, C_KR0:C_KR0 + LANES] * cos_t
            + z[:, C_KR0 + LANES:C_KR0 + 2 * LANES] * sin_t)
    kr = (kr_f.astype(BF16), pltpu.roll(kr_f, ROPE, axis=1).astype(BF16))
    for pair in range(PAIRS):
        c = pair * LANES
        qr = (q[:, Q_ROPE0 + c:Q_ROPE0 + c + LANES] * cos_t
              + q[:, Q_PART0 + c:Q_PART0 + c + LANES] * sin_t)
        qr = (qr * scale).astype(BF16)
        for h in (2 * pair, 2 * pair + 1):
            c = h * QK_PAD
            q_ref[h, rws, :LANES] = (q[:, h * NOPE:(h + 1) * NOPE] * scale).astype(BF16)
            q_ref[h, rws, LANES:] = qr
            k_ref[h, rws, :LANES] = kv[:, c:c + LANES].astype(BF16)
            k_ref[h, rws, LANES:] = kr[h % 2]
            vt_ref[h, :, rws] = kv[:, c + LANES:c + QK_PAD].T.astype(BF16)

    for g in range(GROUPS):
        cg = g * GROUP_DIM
        vg = jax.nn.gelu(z[:, C_V0 + cg:C_V0 + cg + GROUP_DIM])
        vn = _rms(vg, gv_ref[:, cg:cg + GROUP_DIM]).astype(BF16)
        ug = jax.nn.gelu(z[:, C_U0 + cg:C_U0 + cg + GROUP_DIM])
        bias = bs_ref[:, g:g + 1]
        w = ws_ref[g]
        for n in range(0, MIX_SUB, 2 * CHUNK):
            r0, r1 = slice(n, n + CHUNK), slice(n + CHUNK, n + 2 * CHUNK)
            s = _dot(w, jnp.concatenate([vn[r0], vn[r1]], axis=1))
            t0 = rws.start + n
            og_scr[t0:t0 + CHUNK, cg:cg + GROUP_DIM] = ug[r0] * (s[:, :GROUP_DIM] + bias)
            og_scr[t0 + CHUNK:t0 + 2 * CHUNK, cg:cg + GROUP_DIM] = ug[r1] * (s[:, GROUP_DIM:] + bias)
    og_ref[rws, :] = _rms(og_scr[rws, :], gog_ref[...]).astype(BF16)


def _mix(x1, seq, g_mix, w_in, g_q, w_q, g_kv, w_kv, tab, g_v, w_s, b_t, g_og):
    t = x1.shape[0]
    tm = MIX_TM
    blocks_per_seq = seq // tm
    row = lambda w: pl.BlockSpec((tm, w), lambda i: (i, 0))
    return pl.pallas_call(
        _mix_body,
        grid=(t // tm,),
        in_specs=[row(D_MODEL), _resident((1, D_MODEL)), _resident((D_MODEL, IN_COLS)),
                  _resident((1, Q_LORA)), _resident((Q_LORA, Q_COLS)),
                  _resident((1, KV_LORA)), _resident((KV_LORA, HEADS * QK_PAD)),
                  pl.BlockSpec((tm, 2 * LANES), lambda i: (i % blocks_per_seq, 0)),
                  _resident((1, GMLP_WIDTH)), _resident((GROUPS, CHUNK, CHUNK)),
                  _resident((CHUNK, GROUPS)), _resident((1, GMLP_WIDTH))],
        out_specs=[pl.BlockSpec((HEADS, tm, QK_PAD), lambda i: (0, i, 0)),
                   pl.BlockSpec((HEADS, tm, QK_PAD), lambda i: (0, i, 0)),
                   pl.BlockSpec((HEADS, V_DIM, tm), lambda i: (0, 0, i)), row(GMLP_WIDTH)],
        out_shape=[jax.ShapeDtypeStruct((HEADS, t, QK_PAD), BF16),
                   jax.ShapeDtypeStruct((HEADS, t, QK_PAD), BF16),
                   jax.ShapeDtypeStruct((HEADS, V_DIM, t), BF16),
                   jax.ShapeDtypeStruct((t, GMLP_WIDTH), BF16)],
        scratch_shapes=[pltpu.VMEM((tm, GMLP_WIDTH), F32)],
        compiler_params=_params("parallel"),
        name="mix",
    )(x1, g_mix, w_in, g_q, w_q, g_kv, w_kv, tab, g_v, w_s, b_t, g_og)


def _attn_body(nsub, tk, q_ref, k_ref, vt_ref, o_ref, s_scr):
    heads = vt_ref.shape[0]
    tq = q_ref.shape[1] // nsub
    seq = k_ref.shape[1]
    nchunk = seq // tk
    ahead = min(ATTN_AHEAD, nchunk)
    units = [(h, i) for h in range(heads) for i in range(nsub)]

    def fold(x, op):
        return functools.reduce(op, [x[r:r + SUBLANES] for r in range(0, tk, SUBLANES)])

    def score_chunk(u, c):
        h, i = units[u]
        rows = slice(c * tk, (c + 1) * tk)
        s = lax.dot_general(k_ref[h, rows, :], q_ref[h, i * tq:(i + 1) * tq, :],
                            (((1,), (1,)), ((), ())), preferred_element_type=F32)
        s_scr[u % 2, rows, :] = s
        return fold(s, jnp.maximum)

    maxes = [score_chunk(0, c) for c in range(nchunk)]
    for u, (h, i) in enumerate(units):
        m = jnp.max(functools.reduce(jnp.maximum, maxes), axis=0, keepdims=True)
        more = u + 1 < len(units)
        maxes = [score_chunk(u + 1, c) for c in range(ahead)] if more else []
        lsum = acc = None
        for c in range(nchunk):
            rows = slice(c * tk, (c + 1) * tk)
            if more and c + ahead < nchunk:
                maxes.append(score_chunk(u + 1, c + ahead))
            p = jnp.exp2(s_scr[u % 2, rows, :] - m)
            ps = fold(p, jnp.add)
            pv = _dot(vt_ref[h, :, rows], p.astype(BF16))
            lsum = ps if lsum is None else lsum + ps
            acc = pv if acc is None else acc + pv
        o_t = acc / jnp.sum(lsum, axis=0, keepdims=True)
        o_ref[h, i * tq:(i + 1) * tq, :] = o_t.T


def _attn(q, k, v, batch, seq):
    tq, nsub, tk, hp = ATTN_BLOCKING[seq]
    rows = tq * nsub
    nq = seq // rows
    return pl.pallas_call(
        functools.partial(_attn_body, nsub, tk),
        grid=(batch, HEADS // hp, nq),
        in_specs=[pl.BlockSpec((hp, rows, QK_PAD), lambda b, h, i: (h, b * nq + i, 0)),
                  pl.BlockSpec((hp, seq, QK_PAD), lambda b, h, i: (h, b, 0)),
                  pl.BlockSpec((hp, V_DIM, seq), lambda b, h, i: (h, 0, b))],
        out_specs=pl.BlockSpec((hp, rows, V_DIM), lambda b, h, i: (h, b * nq + i, 0)),
        out_shape=jax.ShapeDtypeStruct((HEADS, batch * seq, V_DIM), F32),
        scratch_shapes=[pltpu.VMEM((2, seq, tq), F32)],
        compiler_params=_params("parallel", "parallel", "arbitrary"),
        name="attn",
    )(q, k, v)


def _merge_body(x_ref, oa_ref, og_ref, goa_ref, wo_ref, o_ref):
    oa = jnp.concatenate([oa_ref[h] for h in range(HEADS)], axis=1)
    on = _rms(oa, goa_ref[...]).astype(BF16)
    o_ref[...] = (x_ref[...] + _dot(on, wo_ref[:MLA_WIDTH, :])
                  + _dot(og_ref[...], wo_ref[MLA_WIDTH:, :]))


def _merge(x1, oa, og, g_oa, w_out):
    t = x1.shape[0]
    tm = MERGE_TM
    row = lambda w: pl.BlockSpec((tm, w), lambda i: (i, 0))
    return pl.pallas_call(
        _merge_body,
        grid=(t // tm,),
        in_specs=[row(D_MODEL), pl.BlockSpec((HEADS, tm, V_DIM), lambda i: (0, i, 0)),
                  row(GMLP_WIDTH),
                  _resident((1, MLA_WIDTH)), _resident((MLA_WIDTH + GMLP_WIDTH, D_MODEL))],
        out_specs=row(D_MODEL),
        out_shape=jax.ShapeDtypeStruct((t, D_MODEL), F32),
        compiler_params=_params("parallel"),
        name="merge",
    )(x1, oa, og, g_oa, w_out)


def _rope_table(seq):
    inv = 1.0 / (ROPE_THETA ** (jnp.arange(0, ROPE, 2, dtype=F32) / ROPE))
    ang = jnp.arange(seq, dtype=F32)[:, None] * inv[None, :]
    reps = LANES // HALF_ROPE
    return jnp.concatenate([jnp.tile(jnp.cos(ang), (1, reps)),
                            jnp.tile(jnp.sin(ang), (1, reps))], axis=1)


def _relay_w_in(w_in):
    r1 = w_in[:, C_KR0:C_KR0 + HALF_ROPE]
    r2 = w_in[:, C_KR0 + HALF_ROPE:C_KR0 + ROPE]
    pad = jnp.zeros((D_MODEL, LANES - ROPE), w_in.dtype)
    return jnp.concatenate(
        [w_in[:, :C_KR0], r1, r2, pad, -r2, r1, pad, w_in[:, C_KR0 + ROPE:]], axis=1).astype(BF16)


def _relay_w_q(w_q):
    w = w_q.reshape(Q_LORA, HEADS, NOPE + ROPE)
    nope, r1, r2 = w[..., :NOPE], w[..., NOPE:NOPE + HALF_ROPE], w[..., NOPE + HALF_ROPE:]
    nope = nope.reshape(Q_LORA, HEADS * NOPE)
    rope = jnp.concatenate([r1, r2], axis=-1).reshape(Q_LORA, PAIRS * LANES)
    partner = jnp.concatenate([-r2, r1], axis=-1).reshape(Q_LORA, PAIRS * LANES)
    return jnp.concatenate([nope, rope, partner], axis=1).astype(BF16)


def _layer(x, seq, w):
    batch = x.shape[0]
    x = x.reshape(batch * seq, D_MODEL)
    x1 = _ffn(x, w["g_ffn1"], w["w1_gate"], w["w1_up"], w["w1_down"])
    q, k, v, og = _mix(x1, seq, w["g_mix"], w["w_in"], w["g_q"], w["w_q"], w["g_kv"], w["w_kv"],
                       _rope_table(seq), w["g_v"], w["w_s"], w["b_t"], w["g_og"])
    oa = _attn(q, k, v, batch, seq)
    x2 = _merge(x1, oa, og, w["g_oa"], w["w_out"])
    y = _ffn(x2, w["g_ffn2"], w["w2_gate"], w["w2_up"], w["w2_down"], w["g_final"])
    return y.reshape(batch, seq, D_MODEL)


def kernel(x_prompt, x_sample, g_ffn1, w1_gate, w1_up, w1_down, g_mix, w_in, g_q, w_q_b, g_kv,
           w_kv_b, g_v, w_s, b_s, g_out_attn, g_out_gmlp, w_out, g_ffn2, w2_gate, w2_up,
           w2_down, g_final):
    row = lambda g: g.reshape(1, -1)
    w = dict(
        g_ffn1=row(g_ffn1[0]), w1_gate=w1_gate[0].astype(BF16), w1_up=w1_up[0].astype(BF16),
        w1_down=w1_down[0].astype(BF16),
        g_mix=row(g_mix[0]), w_in=_relay_w_in(w_in[0]),
        g_q=row(g_q[0]), w_q=_relay_w_q(w_q_b[0]),
        g_kv=row(g_kv[0]), w_kv=w_kv_b[0].astype(BF16),
        g_v=row(g_v[0]), w_s=w_s[0].astype(BF16), b_t=jnp.transpose(b_s[0]),
        g_og=row(g_out_gmlp[0]), g_oa=row(g_out_attn[0]), w_out=w_out[0].astype(BF16),
        g_ffn2=row(g_ffn2[0]), w2_gate=w2_gate[0].astype(BF16), w2_up=w2_up[0].astype(BF16),
        w2_down=w2_down[0].astype(BF16), g_final=row(g_final),
    )
    y_prompt = _layer(x_prompt, x_prompt.shape[1], w)
    y_sample = _layer(x_sample, x_sample.shape[1], w)
    return (y_prompt, y_sample)
```

```python
import functools
import math

import jax
import jax.numpy as jnp
from jax import lax
from jax.experimental import pallas as pl
from jax.experimental.pallas import tpu as pltpu

F32 = jnp.float32
BF16 = jnp.bfloat16

D_MODEL = 2048
HEADS = 8
NOPE = 128
ROPE = 64
HALF_ROPE = ROPE // 2
V_DIM = 128
Q_LORA = 512
KV_LORA = 512
GROUPS = 8
GROUP_DIM = 128
GMLP_WIDTH = GROUPS * GROUP_DIM
MLA_WIDTH = HEADS * V_DIM
CHUNK = 128
D_FF = 5632
ROPE_THETA = 10000.0
EPS = 1e-6

LANES = 128
SUBLANES = 8
BF16_ROWS = 2 * SUBLANES
ATTN_AHEAD = 2
QK_PAD = 2 * LANES
PAIRS = HEADS // 2
C_Q0, C_KV0 = 0, Q_LORA
C_KR0 = Q_LORA + KV_LORA
C_U0 = C_KR0 + 2 * LANES
C_V0 = C_U0 + GMLP_WIDTH
IN_COLS = C_V0 + GMLP_WIDTH
Q_NOPE0, Q_ROPE0, Q_PART0 = 0, HEADS * NOPE, HEADS * NOPE + PAIRS * LANES
Q_COLS = Q_PART0 + PAIRS * LANES

VMEM_LIMIT_BYTES = 60000 * 1024

FFN_TM, FFN_TF = 1024, 512
MIX_TM, MIX_SUB = 512, 256
MERGE_TM = 512
ATTN_BLOCKING = {8192: (256, 8, 1024, 1), 2048: (512, 4, 1024, 2)}


def _rms(x, g):
    return x * lax.rsqrt(jnp.mean(x * x, axis=-1, keepdims=True) + EPS) * g


def _dot(a, b):
    return jnp.dot(a, b, preferred_element_type=F32)


def _params(*sem):
    return pltpu.CompilerParams(dimension_semantics=sem, vmem_limit_bytes=VMEM_LIMIT_BYTES)


def _resident(shape):
    return pl.BlockSpec(shape, lambda *_: (0,) * len(shape), pipeline_mode=pl.Buffered(1))


def _ffn_body(final_norm, x_ref, g_ref, wg_ref, wu_ref, wd_ref, *rest):
    if final_norm:
        gf_ref, o_ref, h_ref = rest
    else:
        o_ref, h_ref = rest
    j = pl.program_id(1)

    @pl.when(j == 0)
    def _():
        x = x_ref[...]
        h_ref[...] = _rms(x, g_ref[...]).astype(BF16)
        o_ref[...] = x

    h = h_ref[...]
    acts = []
    for c in range(0, FFN_TF, FFN_TF // 2):
        cols = slice(c, c + FFN_TF // 2)
        gate = _dot(h, wg_ref[:, cols])
        up = _dot(h, wu_ref[:, cols])
        acts.append((0.5 * gate * jax.nn.sigmoid(gate) * up).astype(BF16))
    o_ref[...] += _dot(jnp.concatenate(acts, axis=1), wd_ref[...])

    if final_norm:
        @pl.when(j == pl.num_programs(1) - 1)
        def _():
            o_ref[...] = _rms(o_ref[...], gf_ref[...])


def _ffn(x, g, wg, wu, wd, g_final=None):
    t = x.shape[0]
    final_norm = g_final is not None
    row = pl.BlockSpec((FFN_TM, D_MODEL), lambda i, j: (i, 0))
    gain = pl.BlockSpec((1, D_MODEL), lambda i, j: (0, 0))
    in_specs = [row, gain,
                pl.BlockSpec((D_MODEL, FFN_TF), lambda i, j: (0, j)),
                pl.BlockSpec((D_MODEL, FFN_TF), lambda i, j: (0, j)),
                pl.BlockSpec((FFN_TF, D_MODEL), lambda i, j: (j, 0))]
    args = [x, g, wg, wu, wd]
    if final_norm:
        in_specs.append(gain)
        args.append(g_final)
    return pl.pallas_call(
        functools.partial(_ffn_body, final_norm),
        grid=(t // FFN_TM, D_FF // FFN_TF),
        in_specs=in_specs,
        out_specs=row,
        out_shape=jax.ShapeDtypeStruct((t, D_MODEL), F32),
        scratch_shapes=[pltpu.VMEM((FFN_TM, D_MODEL), BF16)],
        compiler_params=_params("parallel", "arbitrary"),
        name="ffn_final" if final_norm else "ffn",
    )(*args)


def _mix_body(x_ref, gmix_ref, win_ref, gq_ref, wq_ref, gkv_ref, wkv_ref, tab_ref,
              gv_ref, ws_ref, bs_ref, gog_ref,
              q_ref, k_ref, vt_ref, og_ref, og_scr):
    for t0 in range(0, x_ref.shape[0], MIX_SUB):
        _mix_rows(slice(t0, t0 + MIX_SUB), x_ref, gmix_ref, win_ref, gq_ref, wq_ref, gkv_ref,
                  wkv_ref, tab_ref, gv_ref, ws_ref, bs_ref, gog_ref,
                  q_ref, k_ref, vt_ref, og_ref, og_scr)


def _mix_rows(rws, x_ref, gmix_ref, win_ref, gq_ref, wq_ref, gkv_ref, wkv_ref, tab_ref,
              gv_ref, ws_ref, bs_ref, gog_ref, q_ref, k_ref, vt_ref, og_ref, og_scr):
    scale = math.log2(math.e) / math.sqrt(NOPE + ROPE)
    hm = _rms(x_ref[rws, :], gmix_ref[...]).astype(BF16)
    z = _dot(hm, win_ref[...])
    cq = _rms(z[:, C_Q0:C_Q0 + Q_LORA], gq_ref[...]).astype(BF16)
    ckv = _rms(z[:, C_KV0:C_KV0 + KV_LORA], gkv_ref[...]).astype(BF16)
    q = _dot(cq, wq_ref[...])
    kv = _dot(ckv, wkv_ref[...])
    cos_t = tab_ref[rws, :LANES]
    sin_t = tab_ref[rws, LANES:]
    kr_f = (z[:, C_KR0:C_KR0 + LANES] * cos_t
            + z[:, C_KR0 + LANES:C_KR0 + 2 * LANES] * sin_t)
    kr = (kr_f.astype(BF16), pltpu.roll(kr_f, ROPE, axis=1).astype(BF16))
    for pair in range(PAIRS):
        c = pair * LANES
        qr = (q[:, Q_ROPE0 + c:Q_ROPE0 + c + LANES] * cos_t
              + q[:, Q_PART0 + c:Q_PART0 + c + LANES] * sin_t)
        qr = (qr * scale).astype(BF16)
        for h in (2 * pair, 2 * pair + 1):
            c = h * QK_PAD
            q_ref[h, rws, :LANES] = (q[:, h * NOPE:(h + 1) * NOPE] * scale).astype(BF16)
            q_ref[h, rws, LANES:] = qr
            k_ref[h, rws, :LANES] = kv[:, c:c + LANES].astype(BF16)
            k_ref[h, rws, LANES:] = kr[h % 2]
            vt_ref[h, :, rws] = kv[:, c + LANES:c + QK_PAD].T.astype(BF16)

    for g in range(GROUPS):
        cg = g * GROUP_DIM
        vg = jax.nn.gelu(z[:, C_V0 + cg:C_V0 + cg + GROUP_DIM])
        vn = _rms(vg, gv_ref[:, cg:cg + GROUP_DIM]).astype(BF16)
        ug = jax.nn.gelu(z[:, C_U0 + cg:C_U0 + cg + GROUP_DIM])
        bias = bs_ref[:, g:g + 1]
        w = ws_ref[g]
        for n in range(0, MIX_SUB, 2 * CHUNK):
            r0, r1 = slice(n, n + CHUNK), slice(n + CHUNK, n + 2 * CHUNK)
            s = _dot(w, jnp.concatenate([vn[r0], vn[r1]], axis=1))
            t0 = rws.start + n
            og_scr[t0:t0 + CHUNK, cg:cg + GROUP_DIM] = ug[r0] * (s[:, :GROUP_DIM] + bias)
            og_scr[t0 + CHUNK:t0 + 2 * CHUNK, cg:cg + GROUP_DIM] = ug[r1] * (s[:, GROUP_DIM:] + bias)
    og_ref[rws, :] = _rms(og_scr[rws, :], gog_ref[...]).astype(BF16)


def _mix(x1, seq, g_mix, w_in, g_q, w_q, g_kv, w_kv, tab, g_v, w_s, b_t, g_og):
    t = x1.shape[0]
    tm = MIX_TM
    blocks_per_seq = seq // tm
    row = lambda w: pl.BlockSpec((tm, w), lambda i: (i, 0))
    return pl.pallas_call(
        _mix_body,
        grid=(t // tm,),
        in_specs=[row(D_MODEL), _resident((1, D_MODEL)), _resident((D_MODEL, IN_COLS)),
                  _resident((1, Q_LORA)), _resident((Q_LORA, Q_COLS)),
                  _resident((1, KV_LORA)), _resident((KV_LORA, HEADS * QK_PAD)),
                  pl.BlockSpec((tm, 2 * LANES), lambda i: (i % blocks_per_seq, 0)),
                  _resident((1, GMLP_WIDTH)), _resident((GROUPS, CHUNK, CHUNK)),
                  _resident((CHUNK, GROUPS)), _resident((1, GMLP_WIDTH))],
        out_specs=[pl.BlockSpec((HEADS, tm, QK_PAD), lambda i: (0, i, 0)),
                   pl.BlockSpec((HEADS, tm, QK_PAD), lambda i: (0, i, 0)),
                   pl.BlockSpec((HEADS, V_DIM, tm), lambda i: (0, 0, i)), row(GMLP_WIDTH)],
        out_shape=[jax.ShapeDtypeStruct((HEADS, t, QK_PAD), BF16),
                   jax.ShapeDtypeStruct((HEADS, t, QK_PAD), BF16),
                   jax.ShapeDtypeStruct((HEADS, V_DIM, t), BF16),
                   jax.ShapeDtypeStruct((t, GMLP_WIDTH), BF16)],
        scratch_shapes=[pltpu.VMEM((tm, GMLP_WIDTH), F32)],
        compiler_params=_params("parallel"),
        name="mix",
    )(x1, g_mix, w_in, g_q, w_q, g_kv, w_kv, tab, g_v, w_s, b_t, g_og)


def _attn_body(nsub, tk, ncast, q_ref, k_ref, vt_ref, *rest):
    cast_in, o_ref, cast_out, s_scr = (rest[:ncast], rest[ncast], rest[ncast + 1:-1], rest[-1])
    for src, dst in zip(cast_in, cast_out):
        dst[...] = src[...].astype(BF16)
    heads = vt_ref.shape[0]
    tq = q_ref.shape[1] // nsub
    seq = k_ref.shape[1]
    nchunk = seq // tk
    ahead = min(ATTN_AHEAD, nchunk)
    units = [(h, i) for h in range(heads) for i in range(nsub)]

    def fold(x, op):
        return functools.reduce(op, [x[r:r + SUBLANES] for r in range(0, tk, SUBLANES)])

    def score_chunk(u, c):
        h, i = units[u]
        rows = slice(c * tk, (c + 1) * tk)
        s = lax.dot_general(k_ref[h, rows, :], q_ref[h, i * tq:(i + 1) * tq, :],
                            (((1,), (1,)), ((), ())), preferred_element_type=F32)
        s_scr[u % 2, rows, :] = s
        return fold(s, jnp.maximum)

    maxes = [score_chunk(0, c) for c in range(nchunk)]
    for u, (h, i) in enumerate(units):
        m = jnp.max(functools.reduce(jnp.maximum, maxes), axis=0, keepdims=True)
        more = u + 1 < len(units)
        maxes = [score_chunk(u + 1, c) for c in range(ahead)] if more else []
        lsum = acc = None
        for c in range(nchunk):
            rows = slice(c * tk, (c + 1) * tk)
            if more and c + ahead < nchunk:
                maxes.append(score_chunk(u + 1, c + ahead))
            p = jnp.exp2(s_scr[u % 2, rows, :] - m)
            ps = fold(p, jnp.add)
            pv = _dot(vt_ref[h, :, rows], p.astype(BF16))
            lsum = ps if lsum is None else lsum + ps
            acc = pv if acc is None else acc + pv
        o_t = acc / jnp.sum(lsum, axis=0, keepdims=True)
        o_ref[h, i * tq:(i + 1) * tq, :] = o_t.T


def _attn(q, k, v, batch, seq, cast=()):
    tq, nsub, tk, hp = ATTN_BLOCKING[seq]
    rows = tq * nsub
    nq = seq // rows
    groups = HEADS // hp
    steps = batch * groups * nq

    def slab(w):
        assert w.shape[0] % (steps * BF16_ROWS) == 0, (w.shape, steps)
        return pl.BlockSpec((w.shape[0] // steps, w.shape[1]),
                            lambda b, h, i: ((b * groups + h) * nq + i, 0))

    out = pl.pallas_call(
        functools.partial(_attn_body, nsub, tk, len(cast)),
        grid=(batch, groups, nq),
        in_specs=[pl.BlockSpec((hp, rows, QK_PAD), lambda b, h, i: (h, b * nq + i, 0)),
                  pl.BlockSpec((hp, seq, QK_PAD), lambda b, h, i: (h, b, 0)),
                  pl.BlockSpec((hp, V_DIM, seq), lambda b, h, i: (h, 0, b))]
                 + [slab(w) for w in cast],
        out_specs=[pl.BlockSpec((hp, rows, V_DIM), lambda b, h, i: (h, b * nq + i, 0))]
                  + [slab(w) for w in cast],
        out_shape=[jax.ShapeDtypeStruct((HEADS, batch * seq, V_DIM), F32)]
                  + [jax.ShapeDtypeStruct(w.shape, BF16) for w in cast],
        scratch_shapes=[pltpu.VMEM((2, seq, tq), F32)],
        compiler_params=_params("parallel", "parallel", "arbitrary"),
        name="attn",
    )(q, k, v, *cast)
    return out[0], tuple(out[1:])


def _merge_body(x_ref, oa_ref, og_ref, goa_ref, wo_ref, o_ref):
    oa = jnp.concatenate([oa_ref[h] for h in range(HEADS)], axis=1)
    on = _rms(oa, goa_ref[...]).astype(BF16)
    o_ref[...] = (x_ref[...] + _dot(on, wo_ref[:MLA_WIDTH, :])
                  + _dot(og_ref[...], wo_ref[MLA_WIDTH:, :]))


def _merge(x1, oa, og, g_oa, w_out):
    t = x1.shape[0]
    tm = MERGE_TM
    row = lambda w: pl.BlockSpec((tm, w), lambda i: (i, 0))
    return pl.pallas_call(
        _merge_body,
        grid=(t // tm,),
        in_specs=[row(D_MODEL), pl.BlockSpec((HEADS, tm, V_DIM), lambda i: (0, i, 0)),
                  row(GMLP_WIDTH),
                  _resident((1, MLA_WIDTH)), _resident((MLA_WIDTH + GMLP_WIDTH, D_MODEL))],
        out_specs=row(D_MODEL),
        out_shape=jax.ShapeDtypeStruct((t, D_MODEL), F32),
        compiler_params=_params("parallel"),
        name="merge",
    )(x1, oa, og, g_oa, w_out)


def _rope_table(seq):
    inv = 1.0 / (ROPE_THETA ** (jnp.arange(0, ROPE, 2, dtype=F32) / ROPE))
    ang = jnp.arange(seq, dtype=F32)[:, None] * inv[None, :]
    reps = LANES // HALF_ROPE
    return jnp.concatenate([jnp.tile(jnp.cos(ang), (1, reps)),
                            jnp.tile(jnp.sin(ang), (1, reps))], axis=1)


def _relay_w_in(w_in):
    r1 = w_in[:, C_KR0:C_KR0 + HALF_ROPE]
    r2 = w_in[:, C_KR0 + HALF_ROPE:C_KR0 + ROPE]
    pad = jnp.zeros((D_MODEL, LANES - ROPE), w_in.dtype)
    return jnp.concatenate(
        [w_in[:, :C_KR0], r1, r2, pad, -r2, r1, pad, w_in[:, C_KR0 + ROPE:]], axis=1).astype(BF16)


def _relay_w_q(w_q):
    w = w_q.reshape(Q_LORA, HEADS, NOPE + ROPE)
    nope, r1, r2 = w[..., :NOPE], w[..., NOPE:NOPE + HALF_ROPE], w[..., NOPE + HALF_ROPE:]
    nope = nope.reshape(Q_LORA, HEADS * NOPE)
    rope = jnp.concatenate([r1, r2], axis=-1).reshape(Q_LORA, PAIRS * LANES)
    partner = jnp.concatenate([-r2, r1], axis=-1).reshape(Q_LORA, PAIRS * LANES)
    return jnp.concatenate([nope, rope, partner], axis=1).astype(BF16)


def _layer(x, seq, w, w2, round_w2):
    batch = x.shape[0]
    x = x.reshape(batch * seq, D_MODEL)
    x1 = _ffn(x, w["g_ffn1"], w["w1_gate"], w["w1_up"], w["w1_down"])
    q, k, v, og = _mix(x1, seq, w["g_mix"], w["w_in"], w["g_q"], w["w_q"], w["g_kv"], w["w_kv"],
                       _rope_table(seq), w["g_v"], w["w_s"], w["b_t"], w["g_og"])
    if round_w2:
        oa, w2 = _attn(q, k, v, batch, seq, cast=w2)
    else:
        oa, _ = _attn(q, k, v, batch, seq)
    x2 = _merge(x1, oa, og, w["g_oa"], w["w_out"])
    y = _ffn(x2, w["g_ffn2"], *w2, w["g_final"])
    return y.reshape(batch, seq, D_MODEL), w2


def kernel(x_prompt, x_sample, g_ffn1, w1_gate, w1_up, w1_down, g_mix, w_in, g_q, w_q_b, g_kv,
           w_kv_b, g_v, w_s, b_s, g_out_attn, g_out_gmlp, w_out, g_ffn2, w2_gate, w2_up,
           w2_down, g_final):
    row = lambda g: g.reshape(1, -1)
    w = dict(
        g_ffn1=row(g_ffn1[0]), w1_gate=w1_gate[0].astype(BF16), w1_up=w1_up[0].astype(BF16),
        w1_down=w1_down[0].astype(BF16),
        g_mix=row(g_mix[0]), w_in=_relay_w_in(w_in[0]),
        g_q=row(g_q[0]), w_q=_relay_w_q(w_q_b[0]),
        g_kv=row(g_kv[0]), w_kv=w_kv_b[0].astype(BF16),
        g_v=row(g_v[0]), w_s=w_s[0].astype(BF16), b_t=jnp.transpose(b_s[0]),
        g_og=row(g_out_gmlp[0]), g_oa=row(g_out_attn[0]), w_out=w_out[0].astype(BF16),
        g_ffn2=row(g_ffn2[0]), g_final=row(g_final),
    )
    y_prompt, w2 = _layer(x_prompt, x_prompt.shape[1], w,
                          (w2_gate[0], w2_up[0], w2_down[0]), round_w2=True)
    y_sample, _ = _layer(x_sample, x_sample.shape[1], w, w2, round_w2=False)
    return (y_prompt, y_sample)
```

```python
import functools
import math

import jax
import jax.numpy as jnp
from jax import lax
from jax.experimental import pallas as pl
from jax.experimental.pallas import tpu as pltpu

F32 = jnp.float32
BF16 = jnp.bfloat16

D_MODEL = 2048
HEADS = 8
NOPE = 128
ROPE = 64
HALF_ROPE = ROPE // 2
V_DIM = 128
Q_LORA = 512
KV_LORA = 512
GROUPS = 8
GROUP_DIM = 128
GMLP_WIDTH = GROUPS * GROUP_DIM
MLA_WIDTH = HEADS * V_DIM
CHUNK = 128
D_FF = 5632
ROPE_THETA = 10000.0
EPS = 1e-6

LANES = 128
SUBLANES = 8
BF16_ROWS = 2 * SUBLANES
ATTN_AHEAD = 2
QK_PAD = 2 * LANES
PAIRS = HEADS // 2
C_Q0, C_KV0 = 0, Q_LORA
C_KR0 = Q_LORA + KV_LORA
C_U0 = C_KR0 + 2 * LANES
C_V0 = C_U0 + GMLP_WIDTH
IN_COLS = C_V0 + GMLP_WIDTH
Q_NOPE0, Q_ROPE0, Q_PART0 = 0, HEADS * NOPE, HEADS * NOPE + PAIRS * LANES
Q_COLS = Q_PART0 + PAIRS * LANES

VMEM_LIMIT_BYTES = 60000 * 1024

FFN_TM, FFN_TF = 1024, 512
MIX_TM, MIX_SUB = 512, 256
MERGE_TM = 512
ATTN_BLOCKING = {8192: (256, 8, 1024, 1), 2048: (512, 4, 1024, 2)}


def _rms(x, g):
    return x * lax.rsqrt(jnp.mean(x * x, axis=-1, keepdims=True) + EPS) * g


def _dot(a, b):
    return jnp.dot(a, b, preferred_element_type=F32)


def _params(*sem):
    return pltpu.CompilerParams(dimension_semantics=sem, vmem_limit_bytes=VMEM_LIMIT_BYTES)


def _resident(shape):
    return pl.BlockSpec(shape, lambda *_: (0,) * len(shape), pipeline_mode=pl.Buffered(1))


def _ffn_body(final_norm, x_ref, g_ref, wg_ref, wu_ref, wd_ref, *rest):
    if final_norm:
        gf_ref, o_ref, h_ref = rest
    else:
        o_ref, h_ref = rest
    j = pl.program_id(1)

    @pl.when(j == 0)
    def _():
        x = x_ref[...]
        h_ref[...] = _rms(x, g_ref[...]).astype(BF16)
        o_ref[...] = x

    h = h_ref[...]
    acts = []
    for c in range(0, FFN_TF, FFN_TF // 2):
        cols = slice(c, c + FFN_TF // 2)
        gate = _dot(h, wg_ref[:, cols])
        up = _dot(h, wu_ref[:, cols])
        acts.append((0.5 * gate * jax.nn.sigmoid(gate) * up).astype(BF16))
    o_ref[...] += _dot(jnp.concatenate(acts, axis=1), wd_ref[...])

    if final_norm:
        @pl.when(j == pl.num_programs(1) - 1)
        def _():
            o_ref[...] = _rms(o_ref[...], gf_ref[...])


def _ffn(x, g, wg, wu, wd, g_final=None):
    t = x.shape[0]
    final_norm = g_final is not None
    row = pl.BlockSpec((FFN_TM, D_MODEL), lambda i, j: (i, 0))
    gain = pl.BlockSpec((1, D_MODEL), lambda i, j: (0, 0))
    in_specs = [row, gain,
                pl.BlockSpec((D_MODEL, FFN_TF), lambda i, j: (0, j)),
                pl.BlockSpec((D_MODEL, FFN_TF), lambda i, j: (0, j)),
                pl.BlockSpec((FFN_TF, D_MODEL), lambda i, j: (j, 0))]
    args = [x, g, wg, wu, wd]
    if final_norm:
        in_specs.append(gain)
        args.append(g_final)
    return pl.pallas_call(
        functools.partial(_ffn_body, final_norm),
        grid=(t // FFN_TM, D_FF // FFN_TF),
        in_specs=in_specs,
        out_specs=row,
        out_shape=jax.ShapeDtypeStruct((t, D_MODEL), F32),
        scratch_shapes=[pltpu.VMEM((FFN_TM, D_MODEL), BF16)],
        compiler_params=_params("parallel", "arbitrary"),
        name="ffn_final" if final_norm else "ffn",
    )(*args)


def _mix_body(x_ref, gmix_ref, win_ref, gq_ref, wq_ref, gkv_ref, wkv_ref, tab_ref,
              gv_ref, ws_ref, bs_ref, gog_ref,
              q_ref, k_ref, vt_ref, og_ref, og_scr):
    for t0 in range(0, x_ref.shape[0], MIX_SUB):
        _mix_rows(slice(t0, t0 + MIX_SUB), x_ref, gmix_ref, win_ref, gq_ref, wq_ref, gkv_ref,
                  wkv_ref, tab_ref, gv_ref, ws_ref, bs_ref, gog_ref,
                  q_ref, k_ref, vt_ref, og_ref, og_scr)


def _mix_rows(rws, x_ref, gmix_ref, win_ref, gq_ref, wq_ref, gkv_ref, wkv_ref, tab_ref,
              gv_ref, ws_ref, bs_ref, gog_ref, q_ref, k_ref, vt_ref, og_ref, og_scr):
    scale = math.log2(math.e) / math.sqrt(NOPE + ROPE)
    hm = _rms(x_ref[rws, :], gmix_ref[...]).astype(BF16)
    z = _dot(hm, win_ref[...])
    cq = _rms(z[:, C_Q0:C_Q0 + Q_LORA], gq_ref[...]).astype(BF16)
    ckv = _rms(z[:, C_KV0:C_KV0 + KV_LORA], gkv_ref[...]).astype(BF16)
    q = _dot(cq, wq_ref[...])
    kv = _dot(ckv, wkv_ref[...])
    cos_t = tab_ref[rws, :LANES]
    sin_t = tab_ref[rws, LANES:]
    kr_f = (z[:, C_KR0:C_KR0 + LANES] * cos_t
            + z[:, C_KR0 + LANES:C_KR0 + 2 * LANES] * sin_t)
    kr = (kr_f.astype(BF16), pltpu.roll(kr_f, ROPE, axis=1).astype(BF16))
    for pair in range(PAIRS):
        c = pair * LANES
        qr = (q[:, Q_ROPE0 + c:Q_ROPE0 + c + LANES] * cos_t
              + q[:, Q_PART0 + c:Q_PART0 + c + LANES] * sin_t)
        qr = (qr * scale).astype(BF16)
        for h in (2 * pair, 2 * pair + 1):
            c = h * QK_PAD
            q_ref[h, rws, :LANES] = (q[:, h * NOPE:(h + 1) * NOPE] * scale).astype(BF16)
            q_ref[h, rws, LANES:] = qr
            k_ref[h, rws, :LANES] = kv[:, c:c + LANES].astype(BF16)
            k_ref[h, rws, LANES:] = kr[h % 2]
            vt_ref[h, :, rws] = kv[:, c + LANES:c + QK_PAD].T.astype(BF16)

    for g in range(GROUPS):
        cg = g * GROUP_DIM
        vg = jax.nn.gelu(z[:, C_V0 + cg:C_V0 + cg + GROUP_DIM])
        vn = _rms(vg, gv_ref[:, cg:cg + GROUP_DIM]).astype(BF16)
        ug = jax.nn.gelu(z[:, C_U0 + cg:C_U0 + cg + GROUP_DIM])
        bias = bs_ref[:, g:g + 1]
        w = ws_ref[g]
        for n in range(0, MIX_SUB, 2 * CHUNK):
            r0, r1 = slice(n, n + CHUNK), slice(n + CHUNK, n + 2 * CHUNK)
            s = _dot(w, jnp.concatenate([vn[r0], vn[r1]], axis=1))
            t0 = rws.start + n
            og_scr[t0:t0 + CHUNK, cg:cg + GROUP_DIM] = ug[r0] * (s[:, :GROUP_DIM] + bias)
            og_scr[t0 + CHUNK:t0 + 2 * CHUNK, cg:cg + GROUP_DIM] = ug[r1] * (s[:, GROUP_DIM:] + bias)
    og_ref[rws, :] = _rms(og_scr[rws, :], gog_ref[...]).astype(BF16)


def _mix(x1, seq, g_mix, w_in, g_q, w_q, g_kv, w_kv, tab, g_v, w_s, b_t, g_og):
    t = x1.shape[0]
    tm = MIX_TM
    blocks_per_seq = seq // tm
    row = lambda w: pl.BlockSpec((tm, w), lambda i: (i, 0))
    return pl.pallas_call(
        _mix_body,
        grid=(t // tm,),
        in_specs=[row(D_MODEL), _resident((1, D_MODEL)), _resident((D_MODEL, IN_COLS)),
                  _resident((1, Q_LORA)), _resident((Q_LORA, Q_COLS)),
                  _resident((1, KV_LORA)), _resident((KV_LORA, HEADS * QK_PAD)),
                  pl.BlockSpec((tm, 2 * LANES), lambda i: (i % blocks_per_seq, 0)),
                  _resident((1, GMLP_WIDTH)), _resident((GROUPS, CHUNK, CHUNK)),
                  _resident((CHUNK, GROUPS)), _resident((1, GMLP_WIDTH))],
        out_specs=[pl.BlockSpec((HEADS, tm, QK_PAD), lambda i: (0, i, 0)),
                   pl.BlockSpec((HEADS, tm, QK_PAD), lambda i: (0, i, 0)),
                   pl.BlockSpec((HEADS, V_DIM, tm), lambda i: (0, 0, i)), row(GMLP_WIDTH)],
        out_shape=[jax.ShapeDtypeStruct((HEADS, t, QK_PAD), BF16),
                   jax.ShapeDtypeStruct((HEADS, t, QK_PAD), BF16),
                   jax.ShapeDtypeStruct((HEADS, V_DIM, t), BF16),
                   jax.ShapeDtypeStruct((t, GMLP_WIDTH), BF16)],
        scratch_shapes=[pltpu.VMEM((tm, GMLP_WIDTH), F32)],
        compiler_params=_params("parallel"),
        name="mix",
    )(x1, g_mix, w_in, g_q, w_q, g_kv, w_kv, tab, g_v, w_s, b_t, g_og)


def _attn_body(nsub, tk, ncast, q_ref, k_ref, vt_ref, *rest):
    cast_in, o_ref, cast_out, s_scr = (rest[:ncast], rest[ncast], rest[ncast + 1:-1], rest[-1])
    for src, dst in zip(cast_in, cast_out):
        dst[...] = src[...].astype(BF16)
    heads = vt_ref.shape[0]
    tq = q_ref.shape[1] // nsub
    seq = k_ref.shape[1]
    nchunk = seq // tk
    ahead = min(ATTN_AHEAD, nchunk)
    units = [(h, i) for h in range(heads) for i in range(nsub)]

    def fold(x, op):
        return functools.reduce(op, [x[r:r + SUBLANES] for r in range(0, tk, SUBLANES)])

    def score_chunk(u, c):
        h, i = units[u]
        rows = slice(c * tk, (c + 1) * tk)
        s = lax.dot_general(k_ref[h, rows, :], q_ref[h, i * tq:(i + 1) * tq, :],
                            (((1,), (1,)), ((), ())), preferred_element_type=F32)
        s_scr[u % 2, rows, :] = s
        return fold(s, jnp.maximum)

    maxes = [score_chunk(0, c) for c in range(nchunk)]
    for u, (h, i) in enumerate(units):
        m = jnp.max(functools.reduce(jnp.maximum, maxes), axis=0, keepdims=True)
        more = u + 1 < len(units)
        maxes = [score_chunk(u + 1, c) for c in range(ahead)] if more else []
        lsum = acc = None
        for c in range(nchunk):
            rows = slice(c * tk, (c + 1) * tk)
            if more and c + ahead < nchunk:
                maxes.append(score_chunk(u + 1, c + ahead))
            p = jnp.exp2(s_scr[u % 2, rows, :] - m)
            ps = fold(p, jnp.add)
            pv = _dot(vt_ref[h, :, rows], p.astype(BF16))
            lsum = ps if lsum is None else lsum + ps
            acc = pv if acc is None else acc + pv
        o_t = acc / jnp.sum(lsum, axis=0, keepdims=True)
        o_ref[h, i * tq:(i + 1) * tq, :] = o_t.T


def _attn(q, k, v, batch, seq, cast=()):
    tq, nsub, tk, hp = ATTN_BLOCKING[seq]
    rows = tq * nsub
    nq = seq // rows
    groups = HEADS // hp
    steps = batch * groups * nq

    def slab(w):
        assert w.shape[0] % (steps * BF16_ROWS) == 0, (w.shape, steps)
        return pl.BlockSpec((w.shape[0] // steps, w.shape[1]),
                            lambda b, h, i: ((b * groups + h) * nq + i, 0))

    out = pl.pallas_call(
        functools.partial(_attn_body, nsub, tk, len(cast)),
        grid=(batch, groups, nq),
        in_specs=[pl.BlockSpec((hp, rows, QK_PAD), lambda b, h, i: (h, b * nq + i, 0)),
                  pl.BlockSpec((hp, seq, QK_PAD), lambda b, h, i: (h, b, 0)),
                  pl.BlockSpec((hp, V_DIM, seq), lambda b, h, i: (h, 0, b))]
                 + [slab(w) for w in cast],
        out_specs=[pl.BlockSpec((hp, rows, V_DIM), lambda b, h, i: (h, b * nq + i, 0))]
                  + [slab(w) for w in cast],
        out_shape=[jax.ShapeDtypeStruct((HEADS, batch * seq, V_DIM), F32)]
                  + [jax.ShapeDtypeStruct(w.shape, BF16) for w in cast],
        scratch_shapes=[pltpu.VMEM((2, seq, tq), F32)],
        compiler_params=_params("parallel", "parallel", "arbitrary"),
        name="attn",
    )(q, k, v, *cast)
    return out[0], tuple(out[1:])


def _merge_body(x_ref, oa_ref, og_ref, goa_ref, wo_ref, o_ref):
    oa = jnp.concatenate([oa_ref[h] for h in range(HEADS)], axis=1)
    on = _rms(oa, goa_ref[...]).astype(BF16)
    o_ref[...] = (x_ref[...] + _dot(on, wo_ref[:MLA_WIDTH, :])
                  + _dot(og_ref[...], wo_ref[MLA_WIDTH:, :]))


def _merge(x1, oa, og, g_oa, w_out):
    t = x1.shape[0]
    tm = MERGE_TM
    row = lambda w: pl.BlockSpec((tm, w), lambda i: (i, 0))
    return pl.pallas_call(
        _merge_body,
        grid=(t // tm,),
        in_specs=[row(D_MODEL), pl.BlockSpec((HEADS, tm, V_DIM), lambda i: (0, i, 0)),
                  row(GMLP_WIDTH),
                  _resident((1, MLA_WIDTH)), _resident((MLA_WIDTH + GMLP_WIDTH, D_MODEL))],
        out_specs=row(D_MODEL),
        out_shape=jax.ShapeDtypeStruct((t, D_MODEL), F32),
        compiler_params=_params("parallel"),
        name="merge",
    )(x1, oa, og, g_oa, w_out)


def _rope_table(seq):
    inv = 1.0 / (ROPE_THETA ** (jnp.arange(0, ROPE, 2, dtype=F32) / ROPE))
    inv = jnp.tile(inv, LANES // HALF_ROPE)
    ang = jnp.arange(seq, dtype=F32)[:, None] * inv[None, :]
    return jnp.concatenate([jnp.cos(ang), jnp.sin(ang)], axis=1)


def _relay_w_in(w_in):
    r1 = w_in[:, C_KR0:C_KR0 + HALF_ROPE]
    r2 = w_in[:, C_KR0 + HALF_ROPE:C_KR0 + ROPE]
    pad = jnp.zeros((D_MODEL, LANES - ROPE), w_in.dtype)
    return jnp.concatenate(
        [w_in[:, :C_KR0], r1, r2, pad, -r2, r1, pad, w_in[:, C_KR0 + ROPE:]], axis=1).astype(BF16)


def _relay_w_q(w_q):
    w = w_q.reshape(Q_LORA, HEADS, NOPE + ROPE)
    nope, r1, r2 = w[..., :NOPE], w[..., NOPE:NOPE + HALF_ROPE], w[..., NOPE + HALF_ROPE:]
    nope = nope.reshape(Q_LORA, HEADS * NOPE)
    rope = jnp.concatenate([r1, r2], axis=-1).reshape(Q_LORA, PAIRS * LANES)
    partner = jnp.concatenate([-r2, r1], axis=-1).reshape(Q_LORA, PAIRS * LANES)
    return jnp.concatenate([nope, rope, partner], axis=1).astype(BF16)


def _layer(x, seq, w, late, round_late):
    batch = x.shape[0]
    x = x.reshape(batch * seq, D_MODEL)
    x1 = _ffn(x, w["g_ffn1"], w["w1_gate"], w["w1_up"], w["w1_down"])
    q, k, v, og = _mix(x1, seq, w["g_mix"], w["w_in"], w["g_q"], w["w_q"], w["g_kv"], w["w_kv"],
                       w["rope"], w["g_v"], w["w_s"], w["b_t"], w["g_og"])
    if round_late:
        oa, late = _attn(q, k, v, batch, seq, cast=late)
    else:
        oa, _ = _attn(q, k, v, batch, seq)
    w_out, w2_gate, w2_up, w2_down = late
    x2 = _merge(x1, oa, og, w["g_oa"], w_out)
    y = _ffn(x2, w["g_ffn2"], w2_gate, w2_up, w2_down, w["g_final"])
    return y.reshape(batch, seq, D_MODEL), late


def kernel(x_prompt, x_sample, g_ffn1, w1_gate, w1_up, w1_down, g_mix, w_in, g_q, w_q_b, g_kv,
           w_kv_b, g_v, w_s, b_s, g_out_attn, g_out_gmlp, w_out, g_ffn2, w2_gate, w2_up,
           w2_down, g_final):
    row = lambda g: g.reshape(1, -1)
    w = dict(
        g_ffn1=row(g_ffn1[0]), w1_gate=w1_gate[0].astype(BF16), w1_up=w1_up[0].astype(BF16),
        w1_down=w1_down[0].astype(BF16),
        g_mix=row(g_mix[0]), w_in=_relay_w_in(w_in[0]),
        g_q=row(g_q[0]), w_q=_relay_w_q(w_q_b[0]),
        g_kv=row(g_kv[0]), w_kv=w_kv_b[0].astype(BF16),
        g_v=row(g_v[0]), w_s=w_s[0].astype(BF16), b_t=jnp.transpose(b_s[0]),
        g_og=row(g_out_gmlp[0]), g_oa=row(g_out_attn[0]),
        g_ffn2=row(g_ffn2[0]), g_final=row(g_final),
        rope=_rope_table(max(x_prompt.shape[1], x_sample.shape[1])),
    )
    y_prompt, late = _layer(x_prompt, x_prompt.shape[1], w,
                            (w_out[0], w2_gate[0], w2_up[0], w2_down[0]), round_late=True)
    y_sample, _ = _layer(x_sample, x_sample.shape[1], w, late, round_late=False)
    return (y_prompt, y_sample)
```

```python
import functools
import math

import jax
import jax.numpy as jnp
from jax import lax
from jax.experimental import pallas as pl
from jax.experimental.pallas import tpu as pltpu

F32 = jnp.float32
BF16 = jnp.bfloat16

D_MODEL = 2048
HEADS = 8
NOPE = 128
ROPE = 64
HALF_ROPE = ROPE // 2
V_DIM = 128
Q_LORA = 512
KV_LORA = 512
GROUPS = 8
GROUP_DIM = 128
GMLP_WIDTH = GROUPS * GROUP_DIM
MLA_WIDTH = HEADS * V_DIM
CHUNK = 128
D_FF = 5632
ROPE_THETA = 10000.0
EPS = 1e-6

LANES = 128
SUBLANES = 8
BF16_ROWS = 2 * SUBLANES
ATTN_AHEAD = 2
QK_PAD = 2 * LANES
PAIRS = HEADS // 2
C_Q0, C_KV0 = 0, Q_LORA
C_KR0 = Q_LORA + KV_LORA
C_U0 = C_KR0 + 2 * LANES
C_V0 = C_U0 + GMLP_WIDTH
IN_COLS = C_V0 + GMLP_WIDTH
Q_NOPE0, Q_ROPE0, Q_PART0 = 0, HEADS * NOPE, HEADS * NOPE + PAIRS * LANES
Q_COLS = Q_PART0 + PAIRS * LANES

VMEM_LIMIT_BYTES = 60000 * 1024

FFN_TM, FFN_TF = 1024, 512
MIX_TM, MIX_SUB = 512, 256
MERGE_TM = 512
ATTN_BLOCKING = {8192: (256, 8, 1024, 1), 2048: (512, 4, 1024, 2)}


def _rms(x, g):
    return x * lax.rsqrt(jnp.mean(x * x, axis=-1, keepdims=True) + EPS) * g


def _dot(a, b):
    return jnp.dot(a, b, preferred_element_type=F32)


def _params(*sem):
    return pltpu.CompilerParams(dimension_semantics=sem, vmem_limit_bytes=VMEM_LIMIT_BYTES)


def _resident(shape):
    return pl.BlockSpec(shape, lambda *_: (0,) * len(shape), pipeline_mode=pl.Buffered(1))


def _ffn_body(final_norm, x_ref, g_ref, wg_ref, wu_ref, wd_ref, *rest):
    if final_norm:
        gf_ref, o_ref, h_ref = rest
    else:
        o_ref, h_ref = rest
    j = pl.program_id(1)

    def step(first):
        if first:
            h = _rms(x_ref[...], g_ref[...]).astype(BF16)
            h_ref[...] = h
        else:
            h = h_ref[...]
        acts = []
        for c in range(0, FFN_TF, FFN_TF // 2):
            cols = slice(c, c + FFN_TF // 2)
            gate = _dot(h, wg_ref[:, cols])
            up = _dot(h, wu_ref[:, cols])
            acts.append((0.5 * gate * jax.nn.sigmoid(gate) * up).astype(BF16))
        down = _dot(jnp.concatenate(acts, axis=1), wd_ref[...])
        o_ref[...] = (x_ref[...] if first else o_ref[...]) + down

    pl.when(j == 0)(functools.partial(step, True))
    pl.when(j > 0)(functools.partial(step, False))

    if final_norm:
        @pl.when(j == pl.num_programs(1) - 1)
        def _():
            o_ref[...] = _rms(o_ref[...], gf_ref[...])


def _ffn(x, g, wg, wu, wd, g_final=None):
    t = x.shape[0]
    final_norm = g_final is not None
    row = pl.BlockSpec((FFN_TM, D_MODEL), lambda i, j: (i, 0))
    gain = pl.BlockSpec((1, D_MODEL), lambda i, j: (0, 0))
    in_specs = [row, gain,
                pl.BlockSpec((D_MODEL, FFN_TF), lambda i, j: (0, j)),
                pl.BlockSpec((D_MODEL, FFN_TF), lambda i, j: (0, j)),
                pl.BlockSpec((FFN_TF, D_MODEL), lambda i, j: (j, 0))]
    args = [x, g, wg, wu, wd]
    if final_norm:
        in_specs.append(gain)
        args.append(g_final)
    return pl.pallas_call(
        functools.partial(_ffn_body, final_norm),
        grid=(t // FFN_TM, D_FF // FFN_TF),
        in_specs=in_specs,
        out_specs=row,
        out_shape=jax.ShapeDtypeStruct((t, D_MODEL), F32),
        scratch_shapes=[pltpu.VMEM((FFN_TM, D_MODEL), BF16)],
        compiler_params=_params("parallel", "arbitrary"),
        name="ffn_final" if final_norm else "ffn",
    )(*args)


def _mix_body(x_ref, gmix_ref, win_ref, gq_ref, wq_ref, gkv_ref, wkv_ref, tab_ref,
              gv_ref, ws_ref, bs_ref, gog_ref,
              q_ref, k_ref, vt_ref, og_ref, og_scr):
    for t0 in range(0, x_ref.shape[0], MIX_SUB):
        _mix_rows(slice(t0, t0 + MIX_SUB), x_ref, gmix_ref, win_ref, gq_ref, wq_ref, gkv_ref,
                  wkv_ref, tab_ref, gv_ref, ws_ref, bs_ref, gog_ref,
                  q_ref, k_ref, vt_ref, og_ref, og_scr)


def _mix_rows(rws, x_ref, gmix_ref, win_ref, gq_ref, wq_ref, gkv_ref, wkv_ref, tab_ref,
              gv_ref, ws_ref, bs_ref, gog_ref, q_ref, k_ref, vt_ref, og_ref, og_scr):
    scale = math.log2(math.e) / math.sqrt(NOPE + ROPE)
    hm = _rms(x_ref[rws, :], gmix_ref[...]).astype(BF16)
    z = _dot(hm, win_ref[...])
    cq = _rms(z[:, C_Q0:C_Q0 + Q_LORA], gq_ref[...]).astype(BF16)
    ckv = _rms(z[:, C_KV0:C_KV0 + KV_LORA], gkv_ref[...]).astype(BF16)
    q = _dot(cq, wq_ref[...])
    kv = _dot(ckv, wkv_ref[...])
    cos_t = tab_ref[rws, :LANES]
    sin_t = tab_ref[rws, LANES:]
    kr_f = (z[:, C_KR0:C_KR0 + LANES] * cos_t
            + z[:, C_KR0 + LANES:C_KR0 + 2 * LANES] * sin_t)
    kr = (kr_f.astype(BF16), pltpu.roll(kr_f, ROPE, axis=1).astype(BF16))
    for pair in range(PAIRS):
        c = pair * LANES
        qr = (q[:, Q_ROPE0 + c:Q_ROPE0 + c + LANES] * cos_t
              + q[:, Q_PART0 + c:Q_PART0 + c + LANES] * sin_t)
        qr = (qr * scale).astype(BF16)
        for h in (2 * pair, 2 * pair + 1):
            c = h * QK_PAD
            q_ref[h, rws, :LANES] = (q[:, h * NOPE:(h + 1) * NOPE] * scale).astype(BF16)
            q_ref[h, rws, LANES:] = qr
            k_ref[h, rws, :LANES] = kv[:, c:c + LANES].astype(BF16)
            k_ref[h, rws, LANES:] = kr[h % 2]
            vt_ref[h, :, rws] = kv[:, c + LANES:c + QK_PAD].T.astype(BF16)

    for g in range(GROUPS):
        cg = g * GROUP_DIM
        vg = jax.nn.gelu(z[:, C_V0 + cg:C_V0 + cg + GROUP_DIM])
        vn = _rms(vg, gv_ref[:, cg:cg + GROUP_DIM]).astype(BF16)
        ug = jax.nn.gelu(z[:, C_U0 + cg:C_U0 + cg + GROUP_DIM])
        bias = bs_ref[:, g:g + 1]
        w = ws_ref[g]
        for n in range(0, MIX_SUB, 2 * CHUNK):
            r0, r1 = slice(n, n + CHUNK), slice(n + CHUNK, n + 2 * CHUNK)
            s = _dot(w, jnp.concatenate([vn[r0], vn[r1]], axis=1))
            t0 = rws.start + n
            og_scr[t0:t0 + CHUNK, cg:cg + GROUP_DIM] = ug[r0] * (s[:, :GROUP_DIM] + bias)
            og_scr[t0 + CHUNK:t0 + 2 * CHUNK, cg:cg + GROUP_DIM] = ug[r1] * (s[:, GROUP_DIM:] + bias)
    og_ref[rws, :] = _rms(og_scr[rws, :], gog_ref[...]).astype(BF16)


def _mix(x1, seq, g_mix, w_in, g_q, w_q, g_kv, w_kv, tab, g_v, w_s, b_t, g_og):
    t = x1.shape[0]
    tm = MIX_TM
    blocks_per_seq = seq // tm
    row = lambda w: pl.BlockSpec((tm, w), lambda i: (i, 0))
    return pl.pallas_call(
        _mix_body,
        grid=(t // tm,),
        in_specs=[row(D_MODEL), _resident((1, D_MODEL)), _resident((D_MODEL, IN_COLS)),
                  _resident((1, Q_LORA)), _resident((Q_LORA, Q_COLS)),
                  _resident((1, KV_LORA)), _resident((KV_LORA, HEADS * QK_PAD)),
                  pl.BlockSpec((tm, 2 * LANES), lambda i: (i % blocks_per_seq, 0)),
                  _resident((1, GMLP_WIDTH)), _resident((GROUPS, CHUNK, CHUNK)),
                  _resident((CHUNK, GROUPS)), _resident((1, GMLP_WIDTH))],
        out_specs=[pl.BlockSpec((HEADS, tm, QK_PAD), lambda i: (0, i, 0)),
                   pl.BlockSpec((HEADS, tm, QK_PAD), lambda i: (0, i, 0)),
                   pl.BlockSpec((HEADS, V_DIM, tm), lambda i: (0, 0, i)), row(GMLP_WIDTH)],
        out_shape=[jax.ShapeDtypeStruct((HEADS, t, QK_PAD), BF16),
                   jax.ShapeDtypeStruct((HEADS, t, QK_PAD), BF16),
                   jax.ShapeDtypeStruct((HEADS, V_DIM, t), BF16),
                   jax.ShapeDtypeStruct((t, GMLP_WIDTH), BF16)],
        scratch_shapes=[pltpu.VMEM((tm, GMLP_WIDTH), F32)],
        compiler_params=_params("parallel"),
        name="mix",
    )(x1, g_mix, w_in, g_q, w_q, g_kv, w_kv, tab, g_v, w_s, b_t, g_og)


def _attn_body(nsub, tk, ncast, q_ref, k_ref, vt_ref, *rest):
    cast_in, o_ref, cast_out, s_scr = (rest[:ncast], rest[ncast], rest[ncast + 1:-1], rest[-1])
    for src, dst in zip(cast_in, cast_out):
        dst[...] = src[...].astype(BF16)
    heads = vt_ref.shape[0]
    tq = q_ref.shape[1] // nsub
    seq = k_ref.shape[1]
    nchunk = seq // tk
    ahead = min(ATTN_AHEAD, nchunk)
    units = [(h, i) for h in range(heads) for i in range(nsub)]

    def fold(x, op):
        return functools.reduce(op, [x[r:r + SUBLANES] for r in range(0, tk, SUBLANES)])

    def score_chunk(u, c):
        h, i = units[u]
        rows = slice(c * tk, (c + 1) * tk)
        s = lax.dot_general(k_ref[h, rows, :], q_ref[h, i * tq:(i + 1) * tq, :],
                            (((1,), (1,)), ((), ())), preferred_element_type=F32)
        s_scr[u % 2, rows, :] = s
        return fold(s, jnp.maximum)

    maxes = [score_chunk(0, c) for c in range(nchunk)]
    for u, (h, i) in enumerate(units):
        m = jnp.max(functools.reduce(jnp.maximum, maxes), axis=0, keepdims=True)
        more = u + 1 < len(units)
        maxes = [score_chunk(u + 1, c) for c in range(ahead)] if more else []
        lsum = acc = None
        for c in range(nchunk):
            rows = slice(c * tk, (c + 1) * tk)
            if more and c + ahead < nchunk:
                maxes.append(score_chunk(u + 1, c + ahead))
            p = jnp.exp2(s_scr[u % 2, rows, :] - m)
            ps = fold(p, jnp.add)
            pv = _dot(vt_ref[h, :, rows], p.astype(BF16))
            lsum = ps if lsum is None else lsum + ps
            acc = pv if acc is None else acc + pv
        o_t = acc / jnp.sum(lsum, axis=0, keepdims=True)
        o_ref[h, i * tq:(i + 1) * tq, :] = o_t.T


def _attn(q, k, v, batch, seq, cast=()):
    tq, nsub, tk, hp = ATTN_BLOCKING[seq]
    rows = tq * nsub
    nq = seq // rows
    groups = HEADS // hp
    steps = batch * groups * nq

    def slab(w):
        assert w.shape[0] % (steps * BF16_ROWS) == 0, (w.shape, steps)
        return pl.BlockSpec((w.shape[0] // steps, w.shape[1]),
                            lambda b, h, i: ((b * groups + h) * nq + i, 0))

    out = pl.pallas_call(
        functools.partial(_attn_body, nsub, tk, len(cast)),
        grid=(batch, groups, nq),
        in_specs=[pl.BlockSpec((hp, rows, QK_PAD), lambda b, h, i: (h, b * nq + i, 0)),
                  pl.BlockSpec((hp, seq, QK_PAD), lambda b, h, i: (h, b, 0)),
                  pl.BlockSpec((hp, V_DIM, seq), lambda b, h, i: (h, 0, b))]
                 + [slab(w) for w in cast],
        out_specs=[pl.BlockSpec((hp, rows, V_DIM), lambda b, h, i: (h, b * nq + i, 0))]
                  + [slab(w) for w in cast],
        out_shape=[jax.ShapeDtypeStruct((HEADS, batch * seq, V_DIM), F32)]
                  + [jax.ShapeDtypeStruct(w.shape, BF16) for w in cast],
        scratch_shapes=[pltpu.VMEM((2, seq, tq), F32)],
        compiler_params=_params("parallel", "parallel", "arbitrary"),
        name="attn",
    )(q, k, v, *cast)
    return out[0], tuple(out[1:])


def _merge_body(x_ref, oa_ref, og_ref, goa_ref, wo_ref, o_ref):
    oa = jnp.concatenate([oa_ref[h] for h in range(HEADS)], axis=1)
    on = _rms(oa, goa_ref[...]).astype(BF16)
    o_ref[...] = (x_ref[...] + _dot(on, wo_ref[:MLA_WIDTH, :])
                  + _dot(og_ref[...], wo_ref[MLA_WIDTH:, :]))


def _merge(x1, oa, og, g_oa, w_out):
    t = x1.shape[0]
    tm = MERGE_TM
    row = lambda w: pl.BlockSpec((tm, w), lambda i: (i, 0))
    return pl.pallas_call(
        _merge_body,
        grid=(t // tm,),
        in_specs=[row(D_MODEL), pl.BlockSpec((HEADS, tm, V_DIM), lambda i: (0, i, 0)),
                  row(GMLP_WIDTH),
                  _resident((1, MLA_WIDTH)), _resident((MLA_WIDTH + GMLP_WIDTH, D_MODEL))],
        out_specs=row(D_MODEL),
        out_shape=jax.ShapeDtypeStruct((t, D_MODEL), F32),
        compiler_params=_params("parallel"),
        name="merge",
    )(x1, oa, og, g_oa, w_out)


def _rope_table(seq):
    inv = 1.0 / (ROPE_THETA ** (jnp.arange(0, ROPE, 2, dtype=F32) / ROPE))
    inv = jnp.tile(inv, LANES // HALF_ROPE)
    ang = jnp.arange(seq, dtype=F32)[:, None] * inv[None, :]
    return jnp.concatenate([jnp.cos(ang), jnp.sin(ang)], axis=1)


def _relay_w_in(w_in):
    r1 = w_in[:, C_KR0:C_KR0 + HALF_ROPE]
    r2 = w_in[:, C_KR0 + HALF_ROPE:C_KR0 + ROPE]
    pad = jnp.zeros((D_MODEL, LANES - ROPE), w_in.dtype)
    return jnp.concatenate(
        [w_in[:, :C_KR0], r1, r2, pad, -r2, r1, pad, w_in[:, C_KR0 + ROPE:]], axis=1).astype(BF16)


def _relay_w_q(w_q):
    w = w_q.reshape(Q_LORA, HEADS, NOPE + ROPE)
    nope, r1, r2 = w[..., :NOPE], w[..., NOPE:NOPE + HALF_ROPE], w[..., NOPE + HALF_ROPE:]
    nope = nope.reshape(Q_LORA, HEADS * NOPE)
    rope = jnp.concatenate([r1, r2], axis=-1).reshape(Q_LORA, PAIRS * LANES)
    partner = jnp.concatenate([-r2, r1], axis=-1).reshape(Q_LORA, PAIRS * LANES)
    return jnp.concatenate([nope, rope, partner], axis=1).astype(BF16)


def _layer(x, seq, w, late, round_late):
    batch = x.shape[0]
    x = x.reshape(batch * seq, D_MODEL)
    x1 = _ffn(x, w["g_ffn1"], w["w1_gate"], w["w1_up"], w["w1_down"])
    q, k, v, og = _mix(x1, seq, w["g_mix"], w["w_in"], w["g_q"], w["w_q"], w["g_kv"], w["w_kv"],
                       w["rope"], w["g_v"], w["w_s"], w["b_t"], w["g_og"])
    if round_late:
        oa, late = _attn(q, k, v, batch, seq, cast=late)
    else:
        oa, _ = _attn(q, k, v, batch, seq)
    w_out, w2_gate, w2_up, w2_down = late
    x2 = _merge(x1, oa, og, w["g_oa"], w_out)
    y = _ffn(x2, w["g_ffn2"], w2_gate, w2_up, w2_down, w["g_final"])
    return y.reshape(batch, seq, D_MODEL), late


def kernel(x_prompt, x_sample, g_ffn1, w1_gate, w1_up, w1_down, g_mix, w_in, g_q, w_q_b, g_kv,
           w_kv_b, g_v, w_s, b_s, g_out_attn, g_out_gmlp, w_out, g_ffn2, w2_gate, w2_up,
           w2_down, g_final):
    row = lambda g: g.reshape(1, -1)
    w = dict(
        g_ffn1=row(g_ffn1[0]), w1_gate=w1_gate[0].astype(BF16), w1_up=w1_up[0].astype(BF16),
        w1_down=w1_down[0].astype(BF16),
        g_mix=row(g_mix[0]), w_in=_relay_w_in(w_in[0]),
        g_q=row(g_q[0]), w_q=_relay_w_q(w_q_b[0]),
        g_kv=row(g_kv[0]), w_kv=w_kv_b[0].astype(BF16),
        g_v=row(g_v[0]), w_s=w_s[0].astype(BF16), b_t=jnp.transpose(b_s[0]),
        g_og=row(g_out_gmlp[0]), g_oa=row(g_out_attn[0]),
        g_ffn2=row(g_ffn2[0]), g_final=row(g_final),
        rope=_rope_table(max(x_prompt.shape[1], x_sample.shape[1])),
    )
    y_prompt, late = _layer(x_prompt, x_prompt.shape[1], w,
                            (w_out[0], w2_gate[0], w2_up[0], w2_down[0]), round_late=True)
    y_sample, _ = _layer(x_sample, x_sample.shape[1], w, late, round_late=False)
    return (y_prompt, y_sample)
```
